```python
import math
import jax, jax.numpy as jnp
from jax import lax
import numpy as np

D_MODEL = 1024
BATCH = 2
SEQ = 8192
DEPTH = 4
DEC_BATCH = 128
DEC_SEQ = 4
PAST_LEN = 2048
PAGE_SIZE = 128

N_MIXERS = 4
EPS = 1e-6

SSD_D_INNER = 2 * D_MODEL
SSD_HEAD_DIM = 64
SSD_HEADS = SSD_D_INNER // SSD_HEAD_DIM
SSD_GROUPS = 4
SSD_STATE = 128
SSD_CONV = 4
SSD_CHUNK = 128
SSD_CONV_DIM = SSD_D_INNER + 2 * SSD_GROUPS * SSD_STATE
SSD_IN_DIM = SSD_D_INNER + SSD_CONV_DIM + SSD_HEADS

CNV_DIM = D_MODEL
CNV_WIDTH = 31

SBA_HEAD_DIM = 64
SBA_HEADS = D_MODEL // SBA_HEAD_DIM
SBA_BLOCK = 128
SBA_BIAS_INIT = -6.0

GDN_HEADS = 8
GDN_DK = 128
GDN_DV = 256
GDN_CONV = 4
GDN_CHUNK = 64
GDN_KEY_DIM = GDN_HEADS * GDN_DK
GDN_VAL_DIM = GDN_HEADS * GDN_DV
GDN_QKV_DIM = 2 * GDN_KEY_DIM + GDN_VAL_DIM
GDN_IN_DIM = GDN_QKV_DIM + GDN_VAL_DIM + 2 * GDN_HEADS

FFN_HIDDEN = -(-8 * D_MODEL // (3 * 256)) * 256

kernel_name = 'hybrid_ssd_conformer_stickbreak_gdn_adaln_step'


def rmsnorm(x, g):
    xf = x.astype(jnp.float32)
    y = xf * lax.rsqrt(jnp.mean(xf * xf, axis=-1, keepdims=True) + EPS)
    return (y * g.astype(jnp.float32)).astype(x.dtype)


def layernorm(x, g, b):
    xf = x.astype(jnp.float32)
    mu = jnp.mean(xf, axis=-1, keepdims=True)
    var = jnp.mean(jnp.square(xf - mu), axis=-1, keepdims=True)
    y = (xf - mu) * lax.rsqrt(var + EPS)
    return (y * g.astype(jnp.float32) + b.astype(jnp.float32)).astype(x.dtype)


def l2norm(x):
    xf = x.astype(jnp.float32)
    return xf * lax.rsqrt(jnp.sum(xf * xf, axis=-1, keepdims=True) + EPS)


def causal_conv(x, buf, w, b=None):
    xpad = jnp.concatenate([buf.astype(x.dtype), x], axis=1)
    y = lax.conv_general_dilated(xpad, w[:, None, :].astype(x.dtype), window_strides=(1,),
                                 padding='VALID', dimension_numbers=('NWC', 'WIO', 'NWC'),
                                 feature_group_count=x.shape[-1])
    if b is not None:
        y = y + b.astype(x.dtype)
    return y, xpad[:, xpad.shape[1] - (w.shape[0] - 1):]


def adaln(c, w, b):
    m = jax.nn.silu(c) @ w + b
    return jnp.split(m[:, None, :], 6, axis=-1)


def swiglu(h, w13, w2):
    a, u = jnp.split(h @ w13, 2, axis=-1)
    return (jax.nn.silu(a) * u) @ w2


def ssd_chunked(x, dt, A, Bm, Cm, state0):
    b, l, g, r, p = x.shape
    n = Bm.shape[-1]
    T = math.gcd(l, SSD_CHUNK)
    nc = l // T
    X = (x * dt[..., None]).reshape(b, nc, T, g, r, p)
    a = (dt * A).reshape(b, nc, T, g, r)
    Bc = Bm.reshape(b, nc, T, g, n)
    Cc = Cm.reshape(b, nc, T, g, n)
    a_cum = jnp.cumsum(a, axis=2)
    causal = jnp.tril(jnp.ones((T, T), bool))[None, None, :, :, None, None]
    seg = a_cum[:, :, :, None] - a_cum[:, :, None, :]
    Lmat = jnp.exp(jnp.where(causal, seg, -jnp.inf))
    CB = jnp.einsum('bclgn,bcsgn->bcgls', Cc, Bc)
    y_diag = jnp.einsum('bcgls,bclsgr,bcsgrp->bclgrp', CB, Lmat, X)
    decay_to_end = jnp.exp(a_cum[:, :, -1:] - a_cum)
    chunk_states = jnp.einsum('bclgn,bclgr,bclgrp->bcgrpn', Bc, decay_to_end, X)
    chunk_decay = jnp.exp(a_cum[:, :, -1])

    def step(s, inp):
        cs, cd = inp
        return s * cd[..., None, None] + cs, s

    final, states_in = lax.scan(step, state0,
                                (jnp.moveaxis(chunk_states, 1, 0), jnp.moveaxis(chunk_decay, 1, 0)))
    states_in = jnp.moveaxis(states_in, 0, 1)
    y_off = jnp.einsum('bclgn,bcgrpn,bclgr->bclgrp', Cc, states_in, jnp.exp(a_cum))
    return (y_diag + y_off).reshape(b, l, g, r, p), final


def ssd_mixer(h, conv_buf, state0, w_in, conv_w, conv_b, dt_bias, A_log, D_skip, norm_g, w_out):
    f32 = jnp.float32
    b, L, _ = h.shape
    G, R = SSD_GROUPS, SSD_HEADS // SSD_GROUPS
    z, xbc, dt = jnp.split(h @ w_in, [SSD_D_INNER, SSD_D_INNER + SSD_CONV_DIM], axis=-1)
    xbc, new_buf = causal_conv(xbc, conv_buf, conv_w, conv_b)
    xbc = jax.nn.silu(xbc).astype(f32)
    xs, Bm, Cm = jnp.split(xbc, [SSD_D_INNER, SSD_D_INNER + G * SSD_STATE], axis=-1)
    xs = xs.reshape(b, L, G, R, SSD_HEAD_DIM)
    Bm = Bm.reshape(b, L, G, SSD_STATE)
    Cm = Cm.reshape(b, L, G, SSD_STATE)
    dt = jax.nn.softplus(dt.astype(f32) + dt_bias.astype(f32)).reshape(b, L, G, R)
    A = -jnp.exp(A_log.astype(f32)).reshape(G, R)
    s0 = state0.astype(f32).reshape(b, G, R, SSD_HEAD_DIM, SSD_STATE)
    y, state = ssd_chunked(xs, dt, A, Bm, Cm, s0)
    y = y + D_skip.astype(f32).reshape(G, R, 1) * xs
    y = y.reshape(b, L, SSD_D_INNER) * jax.nn.silu(z.astype(f32))
    y = rmsnorm(y.reshape(b, L, G, SSD_D_INNER // G), norm_g.reshape(G, SSD_D_INNER // G))
    out = y.reshape(b, L, SSD_D_INNER).astype(h.dtype) @ w_out
    return out, new_buf, state.reshape(b, SSD_HEADS, SSD_HEAD_DIM, SSD_STATE).astype(h.dtype)


def conv_module(h, conv_buf, w_pw1, b_pw1, dw_w, dw_b, ln_g, ln_b, w_pw2, b_pw2):
    a, gt = jnp.split(h @ w_pw1 + b_pw1, 2, axis=-1)
    u = a * jax.nn.sigmoid(gt)
    y, new_buf = causal_conv(u, conv_buf, dw_w, dw_b)
    y = jax.nn.silu(layernorm(y, ln_g, ln_b))
    return y @ w_pw2 + b_pw2, new_buf


def sba_qkv(h, w_qkv, q_norm_g, k_norm_g):
    b, L, _ = h.shape
    q, k, v = jnp.split(h @ w_qkv, 3, axis=-1)
    q = rmsnorm(q.reshape(b, L, SBA_HEADS, SBA_HEAD_DIM), q_norm_g)
    k = rmsnorm(k.reshape(b, L, SBA_HEADS, SBA_HEAD_DIM), k_norm_g)
    return q, k, v.reshape(b, L, SBA_HEADS, SBA_HEAD_DIM)


def stick_breaking_weights(z, mask):
    log_1m = jnp.where(mask, jax.nn.log_sigmoid(-z), 0.0)
    tail = lax.cumsum(log_1m, axis=z.ndim - 1, reverse=True) - log_1m
    return jnp.where(mask, jnp.exp(jax.nn.log_sigmoid(z) + tail), 0.0)


def sba_prompt(h, w_qkv, q_norm_g, k_norm_g, logit_bias, w_o):
    b, L, _ = h.shape
    q, k, v = sba_qkv(h, w_qkv, q_norm_g, k_norm_g)
    nb = L // SBA_BLOCK
    qb = jnp.moveaxis(q.reshape(b, nb, SBA_BLOCK, SBA_HEADS, SBA_HEAD_DIM), 1, 0)
    kpos = jnp.arange(L)
    scale = SBA_HEAD_DIM ** -0.5
    bias = logit_bias.astype(jnp.float32)[None, :, None, None]

    def block(inp):
        q_blk, start = inp
        qpos = start + jnp.arange(SBA_BLOCK)
        z = jnp.einsum('bqhd,bkhd->bhqk', q_blk, k, preferred_element_type=jnp.float32) * scale + bias
        w = stick_breaking_weights(z, kpos[None, :] < qpos[:, None])
        return jnp.einsum('bhqk,bkhd->bqhd', w.astype(v.dtype), v)

    o = lax.map(block, (qb, jnp.arange(nb) * SBA_BLOCK))
    o = jnp.moveaxis(o, 0, 1).reshape(b, L, SBA_HEADS * SBA_HEAD_DIM)
    return o @ w_o, k, v


def sba_sample(h, cache_k, cache_v, page_table, w_qkv, q_norm_g, k_norm_g, logit_bias, w_o):
    b, L, _ = h.shape
    q, k, v = sba_qkv(h, w_qkv, q_norm_g, k_norm_g)
    P = page_table.shape[1] * PAGE_SIZE
    k_past = cache_k[page_table].reshape(b, P, SBA_HEADS, SBA_HEAD_DIM).astype(h.dtype)
    v_past = cache_v[page_table].reshape(b, P, SBA_HEADS, SBA_HEAD_DIM).astype(h.dtype)
    scale = SBA_HEAD_DIM ** -0.5
    bias = logit_bias.astype(jnp.float32)[None, :, None, None]
    z_past = jnp.einsum('bqhd,bkhd->bhqk', q, k_past, preferred_element_type=jnp.float32) * scale + bias
    z_new = jnp.einsum('bqhd,bkhd->bhqk', q, k, preferred_element_type=jnp.float32) * scale + bias
    qi = jnp.arange(L)
    mask = jnp.concatenate([jnp.ones((L, P), bool), qi[None, :] < qi[:, None]], axis=1)
    w = stick_breaking_weights(jnp.concatenate([z_past, z_new], axis=-1), mask).astype(h.dtype)
    o = (jnp.einsum('bhqk,bkhd->bqhd', w[..., :P], v_past)
         + jnp.einsum('bhqk,bkhd->bqhd', w[..., P:], v))
    return o.reshape(b, L, SBA_HEADS * SBA_HEAD_DIM) @ w_o, k, v


def gated_delta_chunked(q, k, v, beta, g, state0):
    bsz, l, nh, _ = q.shape
    T = math.gcd(l, GDN_CHUNK)
    nc = l // T

    def to_chunks(t):
        return jnp.moveaxis(t.reshape((bsz, nc, T, nh) + t.shape[3:]), 3, 2)

    qc, kc, vc, bc, gc = (to_chunks(t) for t in (q, k, v, beta, g))
    gcum = jnp.cumsum(gc, axis=-1)
    idx = jnp.arange(T)
    incl = idx[:, None] >= idx[None, :]
    strict = idx[:, None] > idx[None, :]
    decay = jnp.exp(jnp.where(incl, gcum[..., :, None] - gcum[..., None, :], -jnp.inf))
    A = jnp.where(strict, bc[..., :, None] * jnp.einsum('bchid,bchjd->bchij', kc, kc) * decay, 0.0)
    eye = jnp.eye(T, dtype=A.dtype)
    Tinv = lax.linalg.triangular_solve(A, jnp.broadcast_to(eye, A.shape), left_side=True,
                                       lower=True, unit_diagonal=True)
    u = jnp.einsum('bchij,bchjd->bchid', Tinv, vc * bc[..., None])
    w = jnp.einsum('bchij,bchjd->bchid', Tinv, kc * (bc * jnp.exp(gcum))[..., None])
    qk = jnp.einsum('bchid,bchjd->bchij', qc, kc) * decay
    q_dec = qc * jnp.exp(gcum)[..., None]
    k_dec = kc * jnp.exp(gcum[..., -1:] - gcum)[..., None]
    chunk_decay = jnp.exp(gcum[..., -1])

    def step(S, inp):
        u_c, w_c, qk_c, qd_c, kd_c, cd_c = inp
        v_new = u_c - jnp.einsum('bhtk,bhkv->bhtv', w_c, S)
        o_c = jnp.einsum('bhtk,bhkv->bhtv', qd_c, S) + jnp.einsum('bhts,bhsv->bhtv', qk_c, v_new)
        S = S * cd_c[..., None, None] + jnp.einsum('bhtk,bhtv->bhkv', kd_c, v_new)
        return S, o_c

    xs = tuple(jnp.moveaxis(t, 1, 0) for t in (u, w, qk, q_dec, k_dec, chunk_decay))
    S, o = lax.scan(step, state0, xs)
    o = jnp.moveaxis(jnp.moveaxis(o, 0, 1), 2, 3).reshape(bsz, l, nh, v.shape[-1])
    return o, S


def gdn_mixer(h, conv_buf, state0, w_in, conv_w, A_log, dt_bias, norm_g, w_out):
    f32 = jnp.float32
    b, L, _ = h.shape
    qkv, z, beta_raw, alpha_raw = jnp.split(
        h @ w_in, [GDN_QKV_DIM, GDN_QKV_DIM + GDN_VAL_DIM, GDN_QKV_DIM + GDN_VAL_DIM + GDN_HEADS], axis=-1)
    qkv, new_buf = causal_conv(qkv, conv_buf, conv_w)
    qkv = jax.nn.silu(qkv)
    q, k, v = jnp.split(qkv, [GDN_KEY_DIM, 2 * GDN_KEY_DIM], axis=-1)
    q = l2norm(q.reshape(b, L, GDN_HEADS, GDN_DK)) * GDN_DK ** -0.5
    k = l2norm(k.reshape(b, L, GDN_HEADS, GDN_DK))
    v = v.reshape(b, L, GDN_HEADS, GDN_DV).astype(f32)
    beta = jax.nn.sigmoid(beta_raw.astype(f32))
    g = -jnp.exp(A_log.astype(f32)) * jax.nn.softplus(alpha_raw.astype(f32) + dt_bias.astype(f32))
    o, state = gated_delta_chunked(q, k, v, beta, g, state0.astype(f32))
    o = rmsnorm(o, norm_g) * jax.nn.silu(z.reshape(b, L, GDN_HEADS, GDN_DV).astype(f32))
    return o.reshape(b, L, GDN_VAL_DIM).astype(h.dtype) @ w_out, new_buf, state.astype(h.dtype)


def setup_inputs(seed: int = 0) -> dict:
    ks = list(jax.random.split(jax.random.key(seed), 64))
    f32 = jnp.float32
    D = D_MODEL
    n_pages = PAST_LEN // PAGE_SIZE
    n_used = DEC_BATCH * n_pages
    n_phys = n_used + max(1, n_used // 4)

    def nrm(shape, scale):
        return jax.random.normal(ks.pop(), shape, f32) * scale

    def gain(shape):
        return 1.0 + nrm(shape, 0.02)

    def dt_bias(n):
        u = jax.random.uniform(ks.pop(), (n,), f32)
        dt = jnp.exp(u * (math.log(0.1) - math.log(0.001)) + math.log(0.001))
        return dt + jnp.log(-jnp.expm1(-dt))

    def a_log(n):
        return jnp.log(jax.random.uniform(ks.pop(), (n,), f32, 1.0, 16.0))

    return {
        'x_prompt': nrm((BATCH, SEQ, D), 1.0),
        'x_sample': nrm((DEC_BATCH, DEC_SEQ, D), 1.0),
        'c_prompt': nrm((BATCH, D), 1.0),
        'c_sample': nrm((DEC_BATCH, D), 1.0),
        'state_ssd_conv': nrm((DEC_BATCH, SSD_CONV - 1, SSD_CONV_DIM), 1.0),
        'state_ssd': nrm((DEC_BATCH, SSD_HEADS, SSD_HEAD_DIM, SSD_STATE), 0.1),
        'state_cnv_conv': nrm((DEC_BATCH, CNV_WIDTH - 1, CNV_DIM), 1.0),
        'cache_k': nrm((n_phys, PAGE_SIZE, SBA_HEADS, SBA_HEAD_DIM), 1.0),
        'cache_v': nrm((n_phys, PAGE_SIZE, SBA_HEADS, SBA_HEAD_DIM), 1.0),
        'page_table': jax.random.permutation(ks.pop(), n_phys)[:n_used].reshape(DEC_BATCH, n_pages).astype(jnp.int32),
        'state_gdn_conv': nrm((DEC_BATCH, GDN_CONV - 1, GDN_QKV_DIM), 1.0),
        'state_gdn': nrm((DEC_BATCH, GDN_HEADS, GDN_DK, GDN_DV), 0.1),
        'ada_w': nrm((DEPTH, D, 6 * D), 0.5 * D ** -0.5),
        'ada_b': nrm((DEPTH, 6 * D), 0.02),
        'norm_mix_g': gain((DEPTH, D)),
        'norm_ffn_g': gain((DEPTH, D)),
        'ffn_w13': nrm((DEPTH, D, 2 * FFN_HIDDEN), D ** -0.5),
        'ffn_w2': nrm((DEPTH, FFN_HIDDEN, D), FFN_HIDDEN ** -0.5),
        'ssd_w_in': nrm((D, SSD_IN_DIM), D ** -0.5),
        'ssd_conv_w': nrm((SSD_CONV, SSD_CONV_DIM), SSD_CONV ** -0.5),
        'ssd_conv_b': nrm((SSD_CONV_DIM,), 0.02),
        'ssd_dt_bias': dt_bias(SSD_HEADS),
        'ssd_A_log': a_log(SSD_HEADS),
        'ssd_D': 1.0 + nrm((SSD_HEADS,), 0.1),
        'ssd_norm_g': gain((SSD_D_INNER,)),
        'ssd_w_out': nrm((SSD_D_INNER, D), SSD_D_INNER ** -0.5),
        'cnv_w_pw1': nrm((D, 2 * CNV_DIM), D ** -0.5),
        'cnv_b_pw1': nrm((2 * CNV_DIM,), 0.02),
        'cnv_dw_w': nrm((CNV_WIDTH, CNV_DIM), CNV_WIDTH ** -0.5),
        'cnv_dw_b': nrm((CNV_DIM,), 0.02),
        'cnv_ln_g': gain((CNV_DIM,)),
        'cnv_ln_b': nrm((CNV_DIM,), 0.02),
        'cnv_w_pw2': nrm((CNV_DIM, D), CNV_DIM ** -0.5),
        'cnv_b_pw2': nrm((D,), 0.02),
        'sba_w_qkv': nrm((D, 3 * SBA_HEADS * SBA_HEAD_DIM), D ** -0.5),
        'sba_q_norm_g': gain((SBA_HEAD_DIM,)),
        'sba_k_norm_g': gain((SBA_HEAD_DIM,)),
        'sba_logit_bias': SBA_BIAS_INIT + nrm((SBA_HEADS,), 0.1),
        'sba_w_o': nrm((SBA_HEADS * SBA_HEAD_DIM, D), (SBA_HEADS * SBA_HEAD_DIM) ** -0.5),
        'gdn_w_in': nrm((D, GDN_IN_DIM), D ** -0.5),
        'gdn_conv_w': nrm((GDN_CONV, GDN_QKV_DIM), GDN_CONV ** -0.5),
        'gdn_A_log': a_log(GDN_HEADS),
        'gdn_dt_bias': dt_bias(GDN_HEADS),
        'gdn_norm_g': gain((GDN_DV,)),
        'gdn_w_out': nrm((GDN_VAL_DIM, D), GDN_VAL_DIM ** -0.5),
    }


def reference(x_prompt, x_sample, c_prompt, c_sample, state_ssd_conv, state_ssd, state_cnv_conv,
              cache_k, cache_v, page_table, state_gdn_conv, state_gdn,
              ada_w, ada_b, norm_mix_g, norm_ffn_g, ffn_w13, ffn_w2,
              ssd_w_in, ssd_conv_w, ssd_conv_b, ssd_dt_bias, ssd_A_log, ssd_D, ssd_norm_g, ssd_w_out,
              cnv_w_pw1, cnv_b_pw1, cnv_dw_w, cnv_dw_b, cnv_ln_g, cnv_ln_b, cnv_w_pw2, cnv_b_pw2,
              sba_w_qkv, sba_q_norm_g, sba_k_norm_g, sba_logit_bias, sba_w_o,
              gdn_w_in, gdn_conv_w, gdn_A_log, gdn_dt_bias, gdn_norm_g, gdn_w_out):
    bp, dtype = x_prompt.shape[0], x_prompt.dtype
    yp, ys = x_prompt, x_sample
    for i in range(DEPTH):
        kind = i % N_MIXERS
        shp, scp, gp, shp2, scp2, gp2 = adaln(c_prompt, ada_w[i], ada_b[i])
        shs, scs, gs, shs2, scs2, gs2 = adaln(c_sample, ada_w[i], ada_b[i])
        hp = rmsnorm(yp, norm_mix_g[i]) * (1 + scp) + shp
        hs = rmsnorm(ys, norm_mix_g[i]) * (1 + scs) + shs
        if kind == 0:
            wts = (ssd_w_in, ssd_conv_w, ssd_conv_b, ssd_dt_bias, ssd_A_log, ssd_D, ssd_norm_g, ssd_w_out)
            buf0 = jnp.zeros((bp, SSD_CONV - 1, SSD_CONV_DIM), dtype)
            st0 = jnp.zeros((bp, SSD_HEADS, SSD_HEAD_DIM, SSD_STATE), dtype)
            op, ssd_conv_p, ssd_state_p = ssd_mixer(hp, buf0, st0, *wts)
            os_, ssd_conv_s, ssd_state_s = ssd_mixer(hs, state_ssd_conv, state_ssd, *wts)
        elif kind == 1:
            wts = (cnv_w_pw1, cnv_b_pw1, cnv_dw_w, cnv_dw_b, cnv_ln_g, cnv_ln_b, cnv_w_pw2, cnv_b_pw2)
            buf0 = jnp.zeros((bp, CNV_WIDTH - 1, CNV_DIM), dtype)
            op, cnv_conv_p = conv_module(hp, buf0, *wts)
            os_, cnv_conv_s = conv_module(hs, state_cnv_conv, *wts)
        elif kind == 2:
            op, k_p, v_p = sba_prompt(hp, sba_w_qkv, sba_q_norm_g, sba_k_norm_g, sba_logit_bias, sba_w_o)
            os_, k_s, v_s = sba_sample(hs, cache_k, cache_v, page_table, sba_w_qkv,
                                       sba_q_norm_g, sba_k_norm_g, sba_logit_bias, sba_w_o)
        else:
            wts = (gdn_w_in, gdn_conv_w, gdn_A_log, gdn_dt_bias, gdn_norm_g, gdn_w_out)
            buf0 = jnp.zeros((bp, GDN_CONV - 1, GDN_QKV_DIM), dtype)
            st0 = jnp.zeros((bp, GDN_HEADS, GDN_DK, GDN_DV), dtype)
            op, gdn_conv_p, gdn_state_p = gdn_mixer(hp, buf0, st0, *wts)
            os_, gdn_conv_s, gdn_state_s = gdn_mixer(hs, state_gdn_conv, state_gdn, *wts)
        yp = yp + gp * op
        ys = ys + gs * os_
        yp = yp + gp2 * swiglu(rmsnorm(yp, norm_ffn_g[i]) * (1 + scp2) + shp2, ffn_w13[i], ffn_w2[i])
        ys = ys + gs2 * swiglu(rmsnorm(ys, norm_ffn_g[i]) * (1 + scs2) + shs2, ffn_w13[i], ffn_w2[i])
    return (yp, ys, ssd_conv_p, ssd_conv_s, ssd_state_p, ssd_state_s, cnv_conv_p, cnv_conv_s,
            k_p, v_p, k_s, v_s, gdn_conv_p, gdn_conv_s, gdn_state_p, gdn_state_s)
```

```python
import functools
import math

import jax
import jax.numpy as jnp
from jax import lax
from jax.experimental import pallas as pl
from jax.experimental.pallas import tpu as pltpu

F32 = jnp.float32
BF16 = jnp.bfloat16
HI = lax.Precision.HIGHEST
EPS = 1e-6
NEG = -1e30

LANES = 128
SUBLANES = 8

SSD_HEAD_DIM = 64
SSD_GROUPS = 4
SSD_STATE = 128
SSD_CONV = 4
SSD_CHUNK = 128
CNV_WIDTH = 31
SBA_HEAD_DIM = 64
SBA_BLOCK = 128
GDN_HEADS = 8
GDN_DK = 128
GDN_DV = 256
GDN_CONV = 4
GDN_CHUNK = 64


def _cparams(sem, vmem_mb=48):
    return pltpu.CompilerParams(dimension_semantics=sem, vmem_limit_bytes=vmem_mb << 20)


def _dot(a, b):
    return jnp.dot(a.astype(BF16), b.astype(BF16), preferred_element_type=F32)


def _dot_nt(a, b):
    return lax.dot_general(a.astype(BF16), b.astype(BF16), (((1,), (1,)), ((), ())),
                           preferred_element_type=F32)


def _dot_tn(a, b):
    return lax.dot_general(a.astype(BF16), b.astype(BF16), (((0,), (0,)), ((), ())),
                           preferred_element_type=F32)


def _dot_hi(a, b):
    return jnp.dot(a, b, precision=HI, preferred_element_type=F32)


def _silu(x):
    return x * jax.nn.sigmoid(x)


def _softplus(x):
    return jnp.maximum(x, 0.0) + jnp.log1p(jnp.exp(-jnp.abs(x)))


def _modnorm(x, g, sc, sh):
    ms = jnp.mean(x * x, axis=-1, keepdims=True)
    return (x * lax.rsqrt(ms + EPS) * g) * (1.0 + sc) + sh


def _iota(shape, dim):
    return lax.broadcasted_iota(jnp.int32, shape, dim)


def _const_spec(shape):
    nd = len(shape)
    return pl.BlockSpec(shape, lambda *_: (0,) * nd, pipeline_mode=pl.Buffered(1))


def _adaln_kernel(c_ref, w_ref, b_ref, o_ref):
    o_ref[0] = _dot(_silu(c_ref[...]), w_ref[0]) + b_ref[0]


def _adaln(c_all, ada_w, ada_b):
    depth, d, n = ada_w.shape
    mc = c_all.shape[0]
    tn = 1024
    return pl.pallas_call(
        _adaln_kernel,
        grid=(depth, n // tn),
        in_specs=[pl.BlockSpec((mc, d), lambda l, j: (0, 0)),
                  pl.BlockSpec((1, d, tn), lambda l, j: (l, 0, j)),
                  pl.BlockSpec((1, 1, tn), lambda l, j: (l, 0, j))],
        out_specs=pl.BlockSpec((1, mc, tn), lambda l, j: (l, 0, j)),
        out_shape=jax.ShapeDtypeStruct((depth, mc, n), F32),
        compiler_params=_cparams(("parallel", "parallel")),
        name="adaln",
    )(c_all, ada_w, ada_b.reshape(depth, 1, n))


def _inproj_kernel(*refs, n_w, has_bias, n_extra, epilogue):
    x_ref, g_ref, sc_ref, sh_ref = refs[:4]
    pos = 4
    w_refs = refs[pos:pos + n_w]
    pos += n_w
    if has_bias:
        b_refs = refs[pos:pos + n_w]
        pos += n_w
    else:
        b_refs = (None,) * n_w
    e_refs = refs[pos:pos + n_extra]
    o_refs = refs[pos + n_extra:]
    h = _modnorm(x_ref[0], g_ref[...], sc_ref[0], sh_ref[0]).astype(BF16)
    accs = []
    for w_ref, b_ref in zip(w_refs, b_refs):
        a = jnp.dot(h, w_ref[...], preferred_element_type=F32)
        if b_ref is not None:
            a = a + b_ref[...]
        accs.append(a)
    outs = epilogue(accs, [e[...] for e in e_refs])
    for o_ref, o in zip(o_refs, outs):
        o_ref[0] = o.astype(o_ref.dtype)


def _mod_spec(m, tm):
    if m.shape[1] == 1:
        return pl.BlockSpec((1, 1, m.shape[2]), lambda b, i: (b, 0, 0))
    return pl.BlockSpec((1, tm, m.shape[2]), lambda b, i: (b, i, 0))


def _inproj(x, g, sc, sh, ws, bs, extras, epilogue, out_defs, tm, name):
    bx, lx, d = x.shape
    tm = min(tm, lx)
    in_specs = [pl.BlockSpec((1, tm, d), lambda b, i: (b, i, 0)), _const_spec((1, d)),
                _mod_spec(sc, tm), _mod_spec(sh, tm)]
    args = [x, g.reshape(1, d), sc, sh]
    for w in ws:
        in_specs.append(_const_spec(w.shape))
        args.append(w)
    if bs is not None:
        for b in bs:
            in_specs.append(_const_spec(b.shape))
            args.append(b)
    for e in extras:
        in_specs.append(_const_spec(e.shape))
        args.append(e)
    out_specs = [pl.BlockSpec((1, tm, n), lambda b, i: (b, i, 0)) for n, _ in out_defs]
    out_shape = [jax.ShapeDtypeStruct((bx, lx, n), dt) for n, dt in out_defs]
    kern = functools.partial(_inproj_kernel, n_w=len(ws), has_bias=bs is not None,
                             n_extra=len(extras), epilogue=epilogue)
    return pl.pallas_call(
        kern, grid=(bx, lx // tm), in_specs=in_specs, out_specs=out_specs, out_shape=out_shape,
        compiler_params=_cparams(("parallel", "parallel"), 56), name=name,
    )(*args)


def _ep_identity(accs, extras):
    return accs


def _ep_glu(accs, extras):
    a, gt = accs
    return [a * jax.nn.sigmoid(gt)]


def _group_mean_sq(x, group):
    x2 = x * x
    hi = x2.astype(BF16)
    lo = (x2 - hi.astype(F32)).astype(BF16)
    gm = (_iota((LANES, LANES), 0) // group == _iota((LANES, LANES), 1) // group)
    gm = jnp.where(gm, 1.0, 0.0).astype(BF16)
    outs = []
    for s in range(x.shape[1] // LANES):
        sl = slice(LANES * s, LANES * (s + 1))
        outs.append(jnp.dot(hi[:, sl], gm, preferred_element_type=F32)
                    + jnp.dot(lo[:, sl], gm, preferred_element_type=F32))
    return jnp.concatenate(outs, axis=1) * (1.0 / group)


def _ep_sba(accs, extras):
    q, k, v = accs
    gq, gk = extras
    qn = q * lax.rsqrt(_group_mean_sq(q, SBA_HEAD_DIM) + EPS) * gq
    kn = k * lax.rsqrt(_group_mean_sq(k, SBA_HEAD_DIM) + EPS) * gk
    return [qn, kn, v, kn, v]


def _post_kernel(x_ref, a_ref, wo_ref, bo_ref, gate_ref, g2_ref, sc2_ref, sh2_ref, gate2_ref,
                 w13_ref, w2_ref, o_ref, *, hidden):
    mix = jnp.dot(a_ref[0].astype(BF16), wo_ref[...], preferred_element_type=F32) + bo_ref[...]
    y1 = x_ref[0] + gate_ref[0] * mix
    h2 = _modnorm(y1, g2_ref[...], sc2_ref[0], sh2_ref[0]).astype(BF16)
    a = jnp.dot(h2, w13_ref[:, :hidden], preferred_element_type=F32)
    u = jnp.dot(h2, w13_ref[:, hidden:], preferred_element_type=F32)
    act = (_silu(a) * u).astype(BF16)
    f = jnp.dot(act, w2_ref[...], preferred_element_type=F32)
    o_ref[0] = y1 + gate2_ref[0] * f


def _post(x, act, wo, bo, gate, g2, sc2, sh2, gate2, w13, w2, tm, name):
    bx, lx, d = x.shape
    tm = min(tm, lx)
    ka = act.shape[-1]
    hidden = w2.shape[0]
    tok = lambda n: pl.BlockSpec((1, tm, n), lambda b, i: (b, i, 0))
    in_specs = [tok(d), tok(ka), _const_spec(wo.shape), _const_spec((1, d)), _mod_spec(gate, tm),
                _const_spec((1, d)), _mod_spec(sc2, tm), _mod_spec(sh2, tm), _mod_spec(gate2, tm),
                _const_spec(w13.shape), _const_spec(w2.shape)]
    return pl.pallas_call(
        functools.partial(_post_kernel, hidden=hidden),
        grid=(bx, lx // tm), in_specs=in_specs, out_specs=tok(d),
        out_shape=jax.ShapeDtypeStruct((bx, lx, d), F32),
        compiler_params=_cparams(("parallel", "parallel"), 56), name=name,
    )(x, act, wo, bo.reshape(1, d), gate, g2.reshape(1, d), sc2, sh2, gate2, w13, w2)


def _stage_rows(dst_ref, row0, src, n_valid, t):
    if n_valid < t:
        dst_ref[row0:row0 + t, :] = jnp.zeros((t, dst_ref.shape[1]), dst_ref.dtype)
    dst_ref[row0:row0 + n_valid, :] = src


def _act_dtype(n_valid, t):
    return BF16 if n_valid == t else F32


def _causal_conv(xp_ref, w_ref, first_row, t, width):
    acc = None
    for k in range(width):
        term = w_ref[k:k + 1, :] * xp_ref[first_row + k:first_row + k + t, :]
        acc = term if acc is None else acc + term
    return acc


def _ssd_core_kernel(z_ref, xbc_ref, dtT_ref, s0_ref, cb0_ref, cw_ref, cbias_ref, dtb_ref, alog_ref,
                     dch_ref, ng_ref, y_ref, st_ref, xp_ref, ybuf_ref, *, T, n_valid, nheads, d_inner):
    c = pl.program_id(1)
    hist = SUBLANES
    ngrp = SSD_GROUPS
    nst = SSD_STATE

    @pl.when(c == 0)
    def _():
        xp_ref[0:hist, :] = cb0_ref[0]
        st_ref[0] = s0_ref[0]

    @pl.when(c > 0)
    def _():
        xp_ref[0:hist, :] = xp_ref[T:T + hist, :]

    _stage_rows(xp_ref, hist, xbc_ref[0], n_valid, T)
    conv = _causal_conv(xp_ref, cw_ref, hist - (SSD_CONV - 1), T, SSD_CONV) + cbias_ref[...]
    xa = _silu(conv)

    dtr = _softplus(dtT_ref[0] + dtb_ref[...])
    if n_valid < T:
        dtr = jnp.where(_iota(dtr.shape, 1) < n_valid, dtr, 0.0)
    a_r = dtr * (-jnp.exp(alog_ref[...]))
    tri_u = jnp.where(_iota((T, T), 0) <= _iota((T, T), 1), 1.0, 0.0)
    acum_r = _dot_hi(a_r, tri_u)
    atot = acum_r[:, T - 1:T]
    ecd = jnp.exp(atot)
    eac_r = jnp.exp(acum_r)
    dte_r = jnp.exp(atot - acum_r)
    rt = jnp.concatenate([dtr, acum_r, eac_r, dte_r], axis=0).T

    lo = _iota((T, LANES), 1) < SSD_HEAD_DIM
    rlo = _iota((LANES, LANES), 0) < SSD_HEAD_DIM
    causal = _iota((T, T), 0) >= _iota((T, T), 1)

    def col(q, h):
        return rt[:, q * nheads + h:q * nheads + h + 1]

    def pair(q, h0):
        return jnp.where(lo, col(q, h0), col(q, h0 + 1))

    hpg = nheads // ngrp
    for g in range(ngrp):
        bg = xa[:, d_inner + nst * g:d_inner + nst * (g + 1)]
        cg = xa[:, d_inner + nst * (ngrp + g):d_inner + nst * (ngrp + g + 1)]
        cb = _dot_nt(cg, bg)
        for jj in range(hpg // 2):
            h0 = g * hpg + 2 * jj
            sl = slice(SSD_HEAD_DIM * h0, SSD_HEAD_DIM * (h0 + 2))
            xs_p = xa[:, sl]
            x = xs_p * pair(0, h0)
            ms = []
            for h in (h0, h0 + 1):
                seg = col(1, h) - acum_r[h:h + 1, :]
                ms.append(cb * jnp.exp(jnp.where(causal, seg, NEG)))
            m = jnp.concatenate(ms, axis=1).astype(BF16)
            xbd = jnp.concatenate([jnp.where(lo, x, 0.0), jnp.where(lo, 0.0, x)], axis=0)
            ydiag = _dot(m, xbd)
            sp = st_ref[0, sl, :]
            yoff = _dot_nt(cg, sp) * pair(2, h0)
            ybuf_ref[:, sl] = ydiag + yoff + dch_ref[:, sl] * xs_p
            upd = _dot_tn(x * pair(3, h0), bg)
            cdp = jnp.where(rlo, ecd[h0:h0 + 1, :], ecd[h0 + 1:h0 + 2, :])
            st_ref[0, sl, :] = sp * cdp + upd

    y = ybuf_ref[0:n_valid, :] * _silu(z_ref[0])
    gw = d_inner // ngrp
    for g in range(ngrp):
        sl = slice(gw * g, gw * (g + 1))
        yg = y[:, sl]
        ms = jnp.mean(yg * yg, axis=-1, keepdims=True)
        y_ref[0, :, sl] = (yg * lax.rsqrt(ms + EPS) * ng_ref[:, sl]).astype(y_ref.dtype)


def _ssd_core(z, xbc, dt_t, s0, cb0, cw, cbias, dtb, alog, dch, ng, n_valid):
    b, l, d_inner = z.shape
    conv_dim = xbc.shape[-1]
    nheads = dt_t.shape[1]
    T = SSD_CHUNK
    nc = max(l // T, 1)
    tv = T if n_valid == T else n_valid
    tok = lambda n: pl.BlockSpec((1, tv, n), lambda i, c: (i, c, 0))
    full = lambda s: pl.BlockSpec((1,) + s, lambda i, c: (i, 0, 0))
    in_specs = [tok(d_inner), tok(conv_dim), pl.BlockSpec((1, nheads, T), lambda i, c: (i, 0, c)),
                full(s0.shape[1:]), full(cb0.shape[1:]), _const_spec(cw.shape), _const_spec(cbias.shape),
                _const_spec(dtb.shape), _const_spec(alog.shape), _const_spec(dch.shape), _const_spec(ng.shape)]
    kern = functools.partial(_ssd_core_kernel, T=T, n_valid=n_valid, nheads=nheads, d_inner=d_inner)
    return pl.pallas_call(
        kern, grid=(b, nc), in_specs=in_specs,
        out_specs=[tok(d_inner), full(s0.shape[1:])],
        out_shape=[jax.ShapeDtypeStruct((b, l, d_inner), _act_dtype(n_valid, T)),
                   jax.ShapeDtypeStruct(s0.shape, F32)],
        scratch_shapes=[pltpu.VMEM((T + SUBLANES, conv_dim), F32), pltpu.VMEM((T, d_inner), F32)],
        compiler_params=_cparams(("parallel", "arbitrary"), 48), name="ssd_core",
    )(z, xbc, dt_t, s0, cb0, cw, cbias, dtb, alog, dch, ng)


def _cnv_core_kernel(u_ref, cb0_ref, w_ref, b_ref, lg_ref, lb_ref, o_ref, xp_ref, *, T, n_valid, hist):
    c = pl.program_id(1)

    @pl.when(c == 0)
    def _():
        xp_ref[0:hist, :] = cb0_ref[0]

    @pl.when(c > 0)
    def _():
        xp_ref[0:hist, :] = xp_ref[T:T + hist, :]

    _stage_rows(xp_ref, hist, u_ref[0], n_valid, T)
    acc = _causal_conv(xp_ref, w_ref, hist - (CNV_WIDTH - 1), T, CNV_WIDTH) + b_ref[...]
    mu = jnp.mean(acc, axis=-1, keepdims=True)
    xc = acc - mu
    var = jnp.mean(xc * xc, axis=-1, keepdims=True)
    y = _silu(xc * lax.rsqrt(var + EPS) * lg_ref[...] + lb_ref[...])
    o_ref[0] = y[0:n_valid].astype(o_ref.dtype)


def _cnv_core(u, cb0, w, b, lg, lb, T, n_valid):
    bsz, l, cdim = u.shape
    hist = cb0.shape[1]
    nc = max(l // T, 1)
    tv = T if n_valid == T else n_valid
    tok = pl.BlockSpec((1, tv, cdim), lambda i, c: (i, c, 0))
    in_specs = [tok, pl.BlockSpec((1, hist, cdim), lambda i, c: (i, 0, 0)), _const_spec(w.shape),
                _const_spec(b.shape), _const_spec(lg.shape), _const_spec(lb.shape)]
    return pl.pallas_call(
        functools.partial(_cnv_core_kernel, T=T, n_valid=n_valid, hist=hist),
        grid=(bsz, nc), in_specs=in_specs, out_specs=tok,
        out_shape=jax.ShapeDtypeStruct((bsz, l, cdim), _act_dtype(n_valid, T)),
        scratch_shapes=[pltpu.VMEM((T + hist, cdim), F32)],
        compiler_params=_cparams(("parallel", "arbitrary")), name="cnv_core",
    )(u, cb0, w, b, lg, lb)


def _log_sigmoid_pair(z):
    lsz = jnp.minimum(z, 0.0) - jnp.log1p(jnp.exp(-jnp.abs(z)))
    return lsz, lsz - z


def _split_bf16(x):
    hi = x.astype(BF16)
    return hi, (x - hi.astype(F32)).astype(BF16)


def _sba_prompt_kernel(bias_ref, q_ref, k_ref, v_ref, o_ref, *, tq):
    hp = pl.program_id(1)
    qi = pl.program_id(2)
    tk = tq
    hd = SBA_HEAD_DIM
    b0 = bias_ref[2 * hp]
    b1 = bias_ref[2 * hp + 1]
    scale = hd ** -0.5
    q = q_ref[0].astype(F32) * scale
    lo_q = _iota((tq, LANES), 1) < hd
    q0 = jnp.where(lo_q, q, 0.0).astype(BF16)
    q1 = jnp.where(lo_q, 0.0, q).astype(BF16)
    lo_k = _iota((tk, LANES), 1) < hd
    tail_mat = jnp.where(_iota((tk, tk), 0) > _iota((tk, tk), 1), 1.0, 0.0).astype(BF16)
    diag_mask = _iota((tq, tk), 1) < _iota((tq, tk), 0)

    def head(qh, bias, kb, carry, mask):
        z = _dot_nt(qh, kb) + bias
        lsz, l1m = _log_sigmoid_pair(z)
        if mask is not None:
            l1m = jnp.where(mask, l1m, 0.0)
        hi, lo = _split_bf16(l1m)
        tail = (jnp.dot(hi, tail_mat, preferred_element_type=F32)
                + jnp.dot(lo, tail_mat, preferred_element_type=F32) + carry)
        w = jnp.exp(lsz + tail)
        if mask is not None:
            w = jnp.where(mask, w, 0.0)
        return w.astype(BF16), tail[:, 0:1] + l1m[:, 0:1]

    def block(start, carry0, carry1, acc, mask):
        kb = k_ref[0, pl.ds(start, tk), :]
        vb = v_ref[0, pl.ds(start, tk), :]
        w0, carry0 = head(q0, b0, kb, carry0, mask)
        w1, carry1 = head(q1, b1, kb, carry1, mask)
        vbd = jnp.concatenate([jnp.where(lo_k, vb, jnp.zeros_like(vb)),
                               jnp.where(lo_k, jnp.zeros_like(vb), vb)], axis=0)
        acc = acc + jnp.dot(jnp.concatenate([w0, w1], axis=1), vbd, preferred_element_type=F32)
        return carry0, carry1, acc

    zc = jnp.zeros((tq, 1), F32)
    state = block(pl.multiple_of(qi * tk, tk), zc, zc, jnp.zeros((tq, LANES), F32), diag_mask)

    def body(it, st):
        start = pl.multiple_of((qi - 1 - it) * tk, tk)
        return block(start, st[0], st[1], st[2], None)

    state = lax.fori_loop(0, qi, body, state)
    o_ref[0] = state[2].astype(o_ref.dtype)


def _sba_prompt(q, k, v, bias):
    b, l, d = q.shape
    tq = SBA_BLOCK
    nhp = d // LANES
    kv_spec = pl.BlockSpec((1, l, LANES), lambda i, h, j: (i, 0, h))
    q_spec = pl.BlockSpec((1, tq, LANES), lambda i, h, j: (i, j, h))
    return pl.pallas_call(
        functools.partial(_sba_prompt_kernel, tq=tq),
        grid=(b, nhp, l // tq),
        in_specs=[pl.BlockSpec(memory_space=pltpu.SMEM), q_spec, kv_spec, kv_spec],
        out_specs=q_spec,
        out_shape=jax.ShapeDtypeStruct((b, l, d), BF16),
        compiler_params=_cparams(("parallel", "parallel", "arbitrary")), name="sba_prompt",
    )(bias, q, k, v)


def _sba_sample_kernel(pt_ref, q_ref, kn_ref, vn_ref, ck_ref, cv_ref, biasc_ref, o_ref,
                       qbd_ref, oacc_ref, carry_ref, knp_ref, vnp_ref, *, nq, npages, page):
    s = pl.program_id(1)
    hd = SBA_HEAD_DIM
    d = q_ref.shape[-1]
    ncol = LANES
    bd_mask = _iota((ncol, d), 0) // SUBLANES == _iota((ncol, d), 1) // hd
    tail_mat = jnp.where(_iota((page, page), 1) > _iota((page, page), 0), 1.0, 0.0).astype(BF16)

    def process(kb, vb, mask):
        z = _dot_nt(kb, qbd_ref[...]) + biasc_ref[...]
        lsz, l1m = _log_sigmoid_pair(z)
        if mask is not None:
            l1m = jnp.where(mask, l1m, 0.0)
        hi, lo = _split_bf16(l1m)
        tail = (jnp.dot(tail_mat, hi, preferred_element_type=F32)
                + jnp.dot(tail_mat, lo, preferred_element_type=F32) + carry_ref[...])
        w = jnp.exp(lsz + tail)
        if mask is not None:
            w = jnp.where(mask, w, 0.0)
        carry_ref[...] = tail[0:1, :] + l1m[0:1, :]
        oacc_ref[...] += _dot(w.T, vb)

    @pl.when(s == 0)
    def _():
        knp_ref[...] = jnp.zeros(knp_ref.shape, F32)
        vnp_ref[...] = jnp.zeros(vnp_ref.shape, F32)
        knp_ref[0:nq, :] = kn_ref[0]
        vnp_ref[0:nq, :] = vn_ref[0]
        oacc_ref[...] = jnp.zeros(oacc_ref.shape, F32)
        oacc_ref[0:nq, :] = q_ref[0].astype(F32)
        q8 = oacc_ref[0:SUBLANES, :] * (hd ** -0.5)
        qt = jnp.concatenate([q8] * (ncol // SUBLANES), axis=0)
        qbd_ref[...] = jnp.where(bd_mask, qt, 0.0).astype(BF16)
        oacc_ref[...] = jnp.zeros(oacc_ref.shape, F32)
        carry_ref[...] = jnp.zeros(carry_ref.shape, F32)
        key_i = _iota((page, ncol), 0)
        slot = _iota((page, ncol), 1) % SUBLANES
        process(knp_ref[...], vnp_ref[...], (key_i < slot) & (key_i < nq))

    @pl.when(s > 0)
    def _():
        process(ck_ref[0], cv_ref[0], None)

    @pl.when(s == npages)
    def _():
        m = jnp.where(bd_mask, oacc_ref[...], 0.0)
        red = m.reshape(ncol // SUBLANES, SUBLANES, d).sum(axis=0)
        oacc_ref[0:SUBLANES, :] = red
        o_ref[0] = oacc_ref[0:nq, :].astype(o_ref.dtype)


def _sba_sample(q, kn, vn, cache_k, cache_v, page_table, bias_cols):
    b, nq, d = q.shape
    npages = page_table.shape[1]
    page = cache_k.shape[1]
    tok = pl.BlockSpec((1, nq, d), lambda i, s, pt: (i, 0, 0))

    def page_map(i, s, pt):
        return (pt[i, npages - jnp.maximum(s, 1)], 0, 0)

    pg = pl.BlockSpec((1, page, d), page_map)
    kern = functools.partial(_sba_sample_kernel, nq=nq, npages=npages, page=page)
    return pl.pallas_call(
        kern,
        grid_spec=pltpu.PrefetchScalarGridSpec(
            num_scalar_prefetch=1, grid=(b, npages + 1),
            in_specs=[tok, tok, tok, pg, pg, pl.BlockSpec((1, LANES), lambda i, s, pt: (0, 0))],
            out_specs=tok,
            scratch_shapes=[pltpu.VMEM((LANES, d), BF16), pltpu.VMEM((LANES, d), F32),
                            pltpu.VMEM((1, LANES), F32), pltpu.VMEM((page, d), F32),
                            pltpu.VMEM((page, d), F32)]),
        out_shape=jax.ShapeDtypeStruct((b, nq, d), F32),
        compiler_params=_cparams(("parallel", "arbitrary")), name="sba_sample",
    )(page_table, q, kn, vn, cache_k, cache_v, bias_cols)


def _gdn_core_kernel(qkv_ref, z_ref, ba_ref, s0_ref, cb0_ref, cw_ref, prm_ref, ng_ref,
                     y_ref, st_ref, xp_ref, obuf_ref, *, T, n_valid):
    c = pl.program_id(1)
    hist = SUBLANES
    nh, dk, dv = GDN_HEADS, GDN_DK, GDN_DV
    kdim = nh * dk
    T2 = 2 * T

    @pl.when(c == 0)
    def _():
        xp_ref[0:hist, :] = cb0_ref[0]
        st_ref[0] = s0_ref[0]

    @pl.when(c > 0)
    def _():
        xp_ref[0:hist, :] = xp_ref[T:T + hist, :]

    _stage_rows(xp_ref, hist, qkv_ref[0], n_valid, T)
    xa = _silu(_causal_conv(xp_ref, cw_ref, hist - (GDN_CONV - 1), T, GDN_CONV))

    npair = nh // 2
    lane2 = _iota((npair, T2), 1)
    beta_r = jax.nn.sigmoid(ba_ref[0, 0, 0:npair, :])
    g_r = -jnp.exp(prm_ref[1]) * _softplus(ba_ref[0, 0, npair:2 * npair, :] + prm_ref[0])
    if n_valid < T:
        valid = lane2 % T < n_valid
        beta_r = jnp.where(valid, beta_r, 0.0)
        g_r = jnp.where(valid, g_r, 0.0)
    same = _iota((T2, T2), 0) // T == _iota((T2, T2), 1) // T
    incl = same & (_iota((T2, T2), 0) >= _iota((T2, T2), 1))
    strict = same & (_iota((T2, T2), 0) > _iota((T2, T2), 1))
    cum_mat = jnp.where(same & (_iota((T2, T2), 0) <= _iota((T2, T2), 1)), 1.0, 0.0)
    g8 = jnp.concatenate([g_r, g_r], axis=0)
    gcum_r = _dot_hi(g8, cum_mat)[0:npair]
    gtot_r = jnp.where(lane2 < T, gcum_r[:, T - 1:T], gcum_r[:, T2 - 1:T2])
    eg_r = jnp.exp(gcum_r)
    rows = [beta_r, gcum_r, eg_r, beta_r * eg_r, jnp.exp(gtot_r - gcum_r)]
    pad = jnp.zeros((T2 - len(rows) * SUBLANES, T2), F32)
    rt = jnp.concatenate([jnp.concatenate([r, r], axis=0) for r in rows] + [pad], axis=0).T

    def col(qi, p):
        return rt[:, qi * SUBLANES + p:qi * SUBLANES + p + 1]

    eye = jnp.where(_iota((T2, T2), 0) == _iota((T2, T2), 1), 1.0, 0.0)

    def l2n(x):
        return x * lax.rsqrt(jnp.sum(x * x, axis=-1, keepdims=True) + EPS)

    for p in range(npair):
        h0 = 2 * p
        qs = jnp.concatenate([l2n(xa[:, dk * h:dk * (h + 1)]) for h in (h0, h0 + 1)], axis=0) * dk ** -0.5
        ks = jnp.concatenate([l2n(xa[:, kdim + dk * h:kdim + dk * (h + 1)]) for h in (h0, h0 + 1)], axis=0)
        vs = jnp.concatenate([xa[:, 2 * kdim + dv * h:2 * kdim + dv * (h + 1)] for h in (h0, h0 + 1)], axis=0)
        beta_c, gc_c, eg_c, beg_c, dte_c = (col(i, p) for i in range(5))
        decay = jnp.exp(jnp.where(incl, gc_c - gcum_r[p:p + 1, :], NEG))
        a = jnp.where(strict, beta_c * _dot_nt(ks, ks) * decay, 0.0)
        tinv = eye - a
        pw = a
        for _ in range(int(math.log2(T)) - 1):
            pw = _dot_hi(pw, pw)
            tinv = _dot_hi(tinv, eye + pw)
        u = _dot(tinv, vs * beta_c)
        w = _dot(tinv, ks * beg_c)
        qk = jnp.where(incl, _dot_nt(qs, ks) * decay, 0.0)
        q_dec = qs * eg_c
        k_dec = ks * dte_c
        vnew = []
        for i, h in enumerate((h0, h0 + 1)):
            rs = slice(T * i, T * (i + 1))
            sh = st_ref[0, dk * h:dk * (h + 1), :]
            vnew.append(u[rs] - _dot(w[rs], sh))
        vnew = jnp.concatenate(vnew, axis=0)
        o_intra = _dot(qk, vnew)
        for i, h in enumerate((h0, h0 + 1)):
            rs = slice(T * i, T * (i + 1))
            sh = st_ref[0, dk * h:dk * (h + 1), :]
            obuf_ref[:, dv * h:dv * (h + 1)] = _dot(q_dec[rs], sh) + o_intra[rs]
            cd = jnp.exp(gcum_r[p:p + 1, T * (i + 1) - 1:T * (i + 1)])
            st_ref[0, dk * h:dk * (h + 1), :] = sh * cd + _dot_tn(k_dec[rs], vnew[rs])

    o = obuf_ref[0:n_valid, :]
    z = z_ref[0]
    for h in range(nh):
        sl = slice(dv * h, dv * (h + 1))
        oh = o[:, sl]
        ms = jnp.mean(oh * oh, axis=-1, keepdims=True)
        y_ref[0, :, sl] = (oh * lax.rsqrt(ms + EPS) * ng_ref[...] * _silu(z[:, sl])).astype(y_ref.dtype)


def _gdn_core(qkv, z, ba_row, s0, cb0, cw, prm, ng, n_valid):
    b, l, qdim = qkv.shape
    vdim = z.shape[-1]
    T = GDN_CHUNK
    nc = max(l // T, 1)
    tv = T if n_valid == T else n_valid
    tok = lambda n: pl.BlockSpec((1, tv, n), lambda i, c: (i, c, 0))
    full = lambda s: pl.BlockSpec((1,) + s, lambda i, c: (i, 0, 0))
    in_specs = [tok(qdim), tok(vdim), pl.BlockSpec((1, 1) + ba_row.shape[2:], lambda i, c: (i, c, 0, 0)),
                full(s0.shape[1:]), full(cb0.shape[1:]), _const_spec(cw.shape), _const_spec(prm.shape),
                _const_spec(ng.shape)]
    return pl.pallas_call(
        functools.partial(_gdn_core_kernel, T=T, n_valid=n_valid),
        grid=(b, nc), in_specs=in_specs, out_specs=[tok(vdim), full(s0.shape[1:])],
        out_shape=[jax.ShapeDtypeStruct((b, l, vdim), _act_dtype(n_valid, T)),
                   jax.ShapeDtypeStruct(s0.shape, F32)],
        scratch_shapes=[pltpu.VMEM((T + SUBLANES, qdim), F32), pltpu.VMEM((T, vdim), F32)],
        compiler_params=_cparams(("parallel", "arbitrary")), name="gdn_core",
    )(qkv, z, ba_row, s0, cb0, cw, prm, ng)


def _hist_pad(buf, rows):
    return jnp.pad(buf, ((0, 0), (rows - buf.shape[1], 0), (0, 0)))


def _pair_rows(x, T):
    b, l, h = x.shape
    nc = l // T
    x = x.reshape(b, nc, T, h // 2, 2)
    return jnp.transpose(x, (0, 1, 3, 4, 2)).reshape(b, nc, h // 2, 2 * T)


def _pad_tokens(x, t):
    return jnp.pad(x, ((0, 0), (0, t - x.shape[1]), (0, 0)))


def kernel(x_prompt, x_sample, c_prompt, c_sample, state_ssd_conv, state_ssd, state_cnv_conv, cache_k, cache_v, page_table, state_gdn_conv, state_gdn, ada_w, ada_b, norm_mix_g, norm_ffn_g, ffn_w13, ffn_w2, ssd_w_in, ssd_conv_w, ssd_conv_b, ssd_dt_bias, ssd_A_log, ssd_D, ssd_norm_g, ssd_w_out, cnv_w_pw1, cnv_b_pw1, cnv_dw_w, cnv_dw_b, cnv_ln_g, cnv_ln_b, cnv_w_pw2, cnv_b_pw2, sba_w_qkv, sba_q_norm_g, sba_k_norm_g, sba_logit_bias, sba_w_o, gdn_w_in, gdn_conv_w, gdn_A_log, gdn_dt_bias, gdn_norm_g, gdn_w_out):
    bp, seq, d = x_prompt.shape
    bs, dseq, _ = x_sample.shape
    depth = ada_w.shape[0]
    ns = bs * dseq
    tm_p = 256
    tm_s = 256
    zero_d = jnp.zeros((d,), F32)

    nc_all = bp + bs
    mc = -(-nc_all // SUBLANES) * SUBLANES
    c_all = jnp.pad(jnp.concatenate([c_prompt, c_sample], axis=0), ((0, mc - nc_all), (0, 0)))
    mod = _adaln(c_all, ada_w, ada_b)

    def mods(layer):
        m = mod[layer]
        mp = m[:bp].reshape(bp, 1, 6, d)
        ms = jnp.broadcast_to(m[bp:nc_all].reshape(bs, 1, 6, d), (bs, dseq, 6, d)).reshape(1, ns, 6, d)
        return [mp[:, :, i] for i in range(6)], [ms[:, :, i] for i in range(6)]

    yp = x_prompt
    ys = x_sample.reshape(1, ns, d)
    outs = {}
    for layer in range(depth):
        kind = layer % 4
        (shp, scp, gp, shp2, scp2, gp2), (shs, scs, gs, shs2, scs2, gs2) = mods(layer)
        gmix = norm_mix_g[layer]
        streams = ((yp, scp, shp, tm_p), (ys, scs, shs, tm_s))
        bo = zero_d
        if kind == 0:
            d_inner = ssd_w_out.shape[0]
            conv_dim = ssd_conv_w.shape[1]
            nheads = ssd_dt_bias.shape[0]
            ws = [ssd_w_in[:, :d_inner].astype(BF16), ssd_w_in[:, d_inner:d_inner + conv_dim].astype(BF16),
                  ssd_w_in[:, d_inner + conv_dim:].astype(BF16)]
            odefs = [(d_inner, F32), (conv_dim, F32), (nheads, F32)]
            (zp, xbcp, dtp), (zs, xbcs, dts) = [
                _inproj(x, gmix, sc, sh, ws, None, [], _ep_identity, odefs, tm, "ssd_in")
                for x, sc, sh, tm in streams]
            prm = (ssd_conv_w, ssd_conv_b.reshape(1, conv_dim), ssd_dt_bias.reshape(nheads, 1),
                   ssd_A_log.reshape(nheads, 1), jnp.repeat(ssd_D, SSD_HEAD_DIM).reshape(1, d_inner),
                   ssd_norm_g.reshape(1, d_inner))
            st_shape = (SSD_HEAD_DIM * nheads, SSD_STATE)
            ap, stp = _ssd_core(zp, xbcp, jnp.swapaxes(dtp, 1, 2), jnp.zeros((bp,) + st_shape, F32),
                                jnp.zeros((bp, SUBLANES, conv_dim), F32), *prm, n_valid=SSD_CHUNK)
            zs, xbcs, dts = (t.reshape(bs, dseq, -1) for t in (zs, xbcs, dts))
            as_, sts = _ssd_core(zs, xbcs, jnp.swapaxes(_pad_tokens(dts, SSD_CHUNK), 1, 2),
                                 state_ssd.reshape((bs,) + st_shape), _hist_pad(state_ssd_conv, SUBLANES),
                                 *prm, n_valid=dseq)
            kc = SSD_CONV - 1
            outs["ssd"] = (xbcp[:, seq - kc:], jnp.concatenate([state_ssd_conv, xbcs], axis=1)[:, -kc:],
                           stp.reshape((bp,) + state_ssd.shape[1:]), sts.reshape(state_ssd.shape))
            wo = ssd_w_out
        elif kind == 1:
            cdim = cnv_dw_w.shape[1]
            ws = [cnv_w_pw1[:, :cdim].astype(BF16), cnv_w_pw1[:, cdim:].astype(BF16)]
            bsl = [cnv_b_pw1[:cdim].reshape(1, cdim), cnv_b_pw1[cdim:].reshape(1, cdim)]
            (up,), (us,) = [_inproj(x, gmix, sc, sh, ws, bsl, [], _ep_glu, [(cdim, F32)], tm, "cnv_in")
                            for x, sc, sh, tm in streams]
            hist = 4 * SUBLANES
            wpad = jnp.pad(cnv_dw_w, ((0, hist - cnv_dw_w.shape[0]), (0, 0)))
            prm = (wpad, cnv_dw_b.reshape(1, cdim), cnv_ln_g.reshape(1, cdim), cnv_ln_b.reshape(1, cdim))
            ap = _cnv_core(up, jnp.zeros((bp, hist, cdim), F32), *prm, T=256, n_valid=256)
            us = us.reshape(bs, dseq, cdim)
            as_ = _cnv_core(us, _hist_pad(state_cnv_conv, hist), *prm, T=SUBLANES, n_valid=dseq)
            kc = CNV_WIDTH - 1
            outs["cnv"] = (up[:, seq - kc:], jnp.concatenate([state_cnv_conv, us], axis=1)[:, -kc:])
            wo, bo = cnv_w_pw2, cnv_b_pw2
        elif kind == 2:
            nh = sba_logit_bias.shape[0]
            ws = [sba_w_qkv[:, i * d:(i + 1) * d].astype(BF16) for i in range(3)]
            ex = [jnp.tile(sba_q_norm_g, nh).reshape(1, d), jnp.tile(sba_k_norm_g, nh).reshape(1, d)]
            odefs = [(d, BF16), (d, F32), (d, F32), (d, BF16), (d, BF16)]
            (qp, kp, vp, kpb, vpb), (qs, ks, vs, _, _) = [
                _inproj(x, gmix, sc, sh, ws, None, ex, _ep_sba, odefs, tm, "sba_in")
                for x, sc, sh, tm in streams]
            ap = _sba_prompt(qp, kpb, vpb, sba_logit_bias)
            qs, ks, vs = (t.reshape(bs, dseq, d) for t in (qs, ks, vs))
            page = cache_k.shape[1]
            as_ = _sba_sample(qs, ks, vs, cache_k.reshape(-1, page, d), cache_v.reshape(-1, page, d),
                              page_table, jnp.repeat(sba_logit_bias, SUBLANES).reshape(1, LANES))
            hshape = (nh, SBA_HEAD_DIM)
            outs["sba"] = (kp.reshape((bp, seq) + hshape), vp.reshape((bp, seq) + hshape),
                           ks.reshape((bs, dseq) + hshape), vs.reshape((bs, dseq) + hshape))
            wo = sba_w_o
        else:
            qdim = gdn_conv_w.shape[1]
            vdim = gdn_w_out.shape[0]
            nh = GDN_HEADS
            ws = [gdn_w_in[:, :qdim].astype(BF16), gdn_w_in[:, qdim:qdim + vdim].astype(BF16),
                  gdn_w_in[:, qdim + vdim:].astype(BF16)]
            odefs = [(qdim, F32), (vdim, F32), (2 * nh, F32)]
            (qkvp, zp, bap), (qkvs, zs, bas) = [
                _inproj(x, gmix, sc, sh, ws, None, [], _ep_identity, odefs, tm, "gdn_in")
                for x, sc, sh, tm in streams]
            T = GDN_CHUNK
            prm_rows = jnp.stack([jnp.repeat(gdn_dt_bias.reshape(nh // 2, 2), T, axis=1),
                                  jnp.repeat(gdn_A_log.reshape(nh // 2, 2), T, axis=1)])

            def ba_rows(ba):
                return jnp.concatenate([_pair_rows(ba[..., :nh], T), _pair_rows(ba[..., nh:], T)], axis=2)

            prm = (gdn_conv_w, prm_rows, gdn_norm_g.reshape(1, GDN_DV))
            st_shape = (nh * GDN_DK, GDN_DV)
            ap, stp = _gdn_core(qkvp, zp, ba_rows(bap), jnp.zeros((bp,) + st_shape, F32),
                                jnp.zeros((bp, SUBLANES, qdim), F32), *prm, n_valid=T)
            qkvs, zs, bas = (t.reshape(bs, dseq, -1) for t in (qkvs, zs, bas))
            as_, sts = _gdn_core(qkvs, zs, ba_rows(_pad_tokens(bas, T)), state_gdn.reshape((bs,) + st_shape),
                                 _hist_pad(state_gdn_conv, SUBLANES), *prm, n_valid=dseq)
            kc = GDN_CONV - 1
            outs["gdn"] = (qkvp[:, seq - kc:], jnp.concatenate([state_gdn_conv, qkvs], axis=1)[:, -kc:],
                           stp.reshape((bp,) + state_gdn.shape[1:]), sts.reshape(state_gdn.shape))
            wo = gdn_w_out
        wo_b = wo.astype(BF16)
        w13 = ffn_w13[layer].astype(BF16)
        w2 = ffn_w2[layer].astype(BF16)
        g2 = norm_ffn_g[layer]
        yp = _post(yp, ap, wo_b, bo, gp, g2, scp2, shp2, gp2, w13, w2, tm_p, "post")
        ys = _post(ys, as_.reshape(1, ns, -1), wo_b, bo, gs, g2, scs2, shs2, gs2, w13, w2, tm_s, "post")

    ssd_conv_p, ssd_conv_s, ssd_state_p, ssd_state_s = outs["ssd"]
    cnv_conv_p, cnv_conv_s = outs["cnv"]
    k_p, v_p, k_s, v_s = outs["sba"]
    gdn_conv_p, gdn_conv_s, gdn_state_p, gdn_state_s = outs["gdn"]
    return (yp, ys.reshape(bs, dseq, d), ssd_conv_p, ssd_conv_s, ssd_state_p, ssd_state_s, cnv_conv_p,
            cnv_conv_s, k_p, v_p, k_s, v_s, gdn_conv_p, gdn_conv_s, gdn_state_p, gdn_state_s)
```

```python
import functools
import math

import jax
import jax.numpy as jnp
from jax import lax
from jax.experimental import pallas as pl
from jax.experimental.pallas import tpu as pltpu

F32 = jnp.float32
BF16 = jnp.bfloat16
HI = lax.Precision.HIGHEST
EPS = 1e-6
NEG = -1e30

LANES = 128
SUBLANES = 8

SSD_HEAD_DIM = 64
SSD_GROUPS = 4
SSD_STATE = 128
SSD_CONV = 4
SSD_CHUNK = 128
CNV_WIDTH = 31
SBA_HEAD_DIM = 64
SBA_BLOCK = 128
GDN_HEADS = 8
GDN_DK = 128
GDN_DV = 256
GDN_CONV = 4
GDN_CHUNK = 64


def _cparams(sem, vmem_mb=48):
    return pltpu.CompilerParams(dimension_semantics=sem, vmem_limit_bytes=vmem_mb << 20)


def _dot(a, b):
    return jnp.dot(a.astype(BF16), b.astype(BF16), preferred_element_type=F32)


def _dot_nt(a, b):
    return lax.dot_general(a.astype(BF16), b.astype(BF16), (((1,), (1,)), ((), ())),
                           preferred_element_type=F32)


def _dot_tn(a, b):
    return lax.dot_general(a.astype(BF16), b.astype(BF16), (((0,), (0,)), ((), ())),
                           preferred_element_type=F32)


def _dot_hi(a, b):
    return jnp.dot(a, b, precision=HI, preferred_element_type=F32)


def _dot_split(a, b):
    a_hi = a.astype(BF16)
    a_lo = (a - a_hi.astype(F32)).astype(BF16)
    b_hi = b.astype(BF16)
    b_lo = (b - b_hi.astype(F32)).astype(BF16)
    return jnp.dot(jnp.concatenate([a_hi, a_hi, a_lo], axis=1), jnp.concatenate([b_hi, b_lo, b_hi], axis=0),
                   preferred_element_type=F32)


def _silu(x):
    return x * jax.nn.sigmoid(x)


def _softplus(x):
    return jnp.maximum(x, 0.0) + jnp.log1p(jnp.exp(-jnp.abs(x)))


def _modnorm(x, g, sc, sh):
    ms = jnp.mean(x * x, axis=-1, keepdims=True)
    return (x * lax.rsqrt(ms + EPS) * g) * (1.0 + sc) + sh


def _iota(shape, dim):
    return lax.broadcasted_iota(jnp.int32, shape, dim)


def _const_spec(shape):
    nd = len(shape)
    return pl.BlockSpec(shape, lambda *_: (0,) * nd, pipeline_mode=pl.Buffered(1))


def _adaln_kernel(c_ref, w_ref, b_ref, o_ref):
    o_ref[0] = _dot(_silu(c_ref[...]), w_ref[0]) + b_ref[0]


def _adaln(c_all, ada_w, ada_b):
    depth, d, n = ada_w.shape
    mc = c_all.shape[0]
    tn = 1024
    return pl.pallas_call(
        _adaln_kernel,
        grid=(depth, n // tn),
        in_specs=[pl.BlockSpec((mc, d), lambda l, j: (0, 0)),
                  pl.BlockSpec((1, d, tn), lambda l, j: (l, 0, j)),
                  pl.BlockSpec((1, 1, tn), lambda l, j: (l, 0, j))],
        out_specs=pl.BlockSpec((1, mc, tn), lambda l, j: (l, 0, j)),
        out_shape=jax.ShapeDtypeStruct((depth, mc, n), F32),
        compiler_params=_cparams(("parallel", "parallel")),
        name="adaln",
    )(c_all, ada_w, ada_b.reshape(depth, 1, n))


def _inproj_kernel(*refs, n_w, has_bias, n_extra, epilogue):
    x_ref, g_ref, sc_ref, sh_ref = refs[:4]
    pos = 4
    w_refs = refs[pos:pos + n_w]
    pos += n_w
    if has_bias:
        b_refs = refs[pos:pos + n_w]
        pos += n_w
    else:
        b_refs = (None,) * n_w
    e_refs = refs[pos:pos + n_extra]
    o_refs = refs[pos + n_extra:]
    h = _modnorm(x_ref[0], g_ref[...], sc_ref[0], sh_ref[0]).astype(BF16)
    accs = []
    for w_ref, b_ref in zip(w_refs, b_refs):
        a = jnp.dot(h, w_ref[...], preferred_element_type=F32)
        if b_ref is not None:
            a = a + b_ref[...]
        accs.append(a)
    outs = epilogue(accs, [e[...] for e in e_refs])
    for o_ref, o in zip(o_refs, outs):
        o_ref[0] = o.astype(o_ref.dtype)


def _mod_spec(m, tm):
    if m.shape[1] == 1:
        return pl.BlockSpec((1, 1, m.shape[2]), lambda b, i: (b, 0, 0))
    return pl.BlockSpec((1, tm, m.shape[2]), lambda b, i: (b, i, 0))


def _inproj(x, g, sc, sh, ws, bs, extras, epilogue, out_defs, tm, name):
    bx, lx, d = x.shape
    tm = min(tm, lx)
    in_specs = [pl.BlockSpec((1, tm, d), lambda b, i: (b, i, 0)), _const_spec((1, d)),
                _mod_spec(sc, tm), _mod_spec(sh, tm)]
    args = [x, g.reshape(1, d), sc, sh]
    for w in ws:
        in_specs.append(_const_spec(w.shape))
        args.append(w)
    if bs is not None:
        for b in bs:
            in_specs.append(_const_spec(b.shape))
            args.append(b)
    for e in extras:
        in_specs.append(_const_spec(e.shape))
        args.append(e)
    out_specs = [pl.BlockSpec((1, tm, n), lambda b, i: (b, i, 0)) for n, _ in out_defs]
    out_shape = [jax.ShapeDtypeStruct((bx, lx, n), dt) for n, dt in out_defs]
    kern = functools.partial(_inproj_kernel, n_w=len(ws), has_bias=bs is not None,
                             n_extra=len(extras), epilogue=epilogue)
    return pl.pallas_call(
        kern, grid=(bx, lx // tm), in_specs=in_specs, out_specs=out_specs, out_shape=out_shape,
        compiler_params=_cparams(("parallel", "parallel"), 56), name=name,
    )(*args)


def _ep_identity(accs, extras):
    return accs


def _ep_glu(accs, extras):
    a, gt = accs
    return [a * jax.nn.sigmoid(gt)]


def _group_mean_sq(x, group):
    x2 = x * x
    hi = x2.astype(BF16)
    lo = (x2 - hi.astype(F32)).astype(BF16)
    gm = (_iota((LANES, LANES), 0) // group == _iota((LANES, LANES), 1) // group)
    gm = jnp.where(gm, 1.0, 0.0).astype(BF16)
    outs = []
    for s in range(x.shape[1] // LANES):
        sl = slice(LANES * s, LANES * (s + 1))
        outs.append(jnp.dot(hi[:, sl], gm, preferred_element_type=F32)
                    + jnp.dot(lo[:, sl], gm, preferred_element_type=F32))
    return jnp.concatenate(outs, axis=1) * (1.0 / group)


def _ep_sba(accs, extras):
    q, k, v = accs
    gq, gk = extras
    qn = q * lax.rsqrt(_group_mean_sq(q, SBA_HEAD_DIM) + EPS) * gq
    kn = k * lax.rsqrt(_group_mean_sq(k, SBA_HEAD_DIM) + EPS) * gk
    return [qn, kn, v, kn, v]


def _post_kernel(x_ref, a_ref, wo_ref, bo_ref, gate_ref, g2_ref, sc2_ref, sh2_ref, gate2_ref,
                 w13_ref, w2_ref, o_ref, *, hidden):
    mix = jnp.dot(a_ref[0].astype(BF16), wo_ref[...], preferred_element_type=F32) + bo_ref[...]
    y1 = x_ref[0] + gate_ref[0] * mix
    h2 = _modnorm(y1, g2_ref[...], sc2_ref[0], sh2_ref[0]).astype(BF16)
    a = jnp.dot(h2, w13_ref[:, :hidden], preferred_element_type=F32)
    u = jnp.dot(h2, w13_ref[:, hidden:], preferred_element_type=F32)
    act = (_silu(a) * u).astype(BF16)
    f = jnp.dot(act, w2_ref[...], preferred_element_type=F32)
    o_ref[0] = y1 + gate2_ref[0] * f


def _post(x, act, wo, bo, gate, g2, sc2, sh2, gate2, w13, w2, tm, name):
    bx, lx, d = x.shape
    tm = min(tm, lx)
    ka = act.shape[-1]
    hidden = w2.shape[0]
    tok = lambda n: pl.BlockSpec((1, tm, n), lambda b, i: (b, i, 0))
    in_specs = [tok(d), tok(ka), _const_spec(wo.shape), _const_spec((1, d)), _mod_spec(gate, tm),
                _const_spec((1, d)), _mod_spec(sc2, tm), _mod_spec(sh2, tm), _mod_spec(gate2, tm),
                _const_spec(w13.shape), _const_spec(w2.shape)]
    return pl.pallas_call(
        functools.partial(_post_kernel, hidden=hidden),
        grid=(bx, lx // tm), in_specs=in_specs, out_specs=tok(d),
        out_shape=jax.ShapeDtypeStruct((bx, lx, d), F32),
        compiler_params=_cparams(("parallel", "parallel"), 56), name=name,
    )(x, act, wo, bo.reshape(1, d), gate, g2.reshape(1, d), sc2, sh2, gate2, w13, w2)


def _stage_rows(dst_ref, row0, src, n_valid, t):
    if n_valid < t:
        dst_ref[row0:row0 + t, :] = jnp.zeros((t, dst_ref.shape[1]), dst_ref.dtype)
    dst_ref[row0:row0 + n_valid, :] = src


def _act_dtype(n_valid, t):
    return BF16 if n_valid == t else F32


def _causal_conv(xp_ref, w_ref, first_row, t, width):
    acc = None
    for k in range(width):
        term = w_ref[k:k + 1, :] * xp_ref[first_row + k:first_row + k + t, :]
        acc = term if acc is None else acc + term
    return acc


def _ssd_core_kernel(z_ref, xbc_ref, dtT_ref, s0_ref, cb0_ref, cw_ref, cbias_ref, dtb_ref, alog_ref,
                     dch_ref, ng_ref, y_ref, st_ref, xp_ref, ybuf_ref, *, T, n_valid, nheads, d_inner):
    c = pl.program_id(1)
    hist = SUBLANES
    ngrp = SSD_GROUPS
    nst = SSD_STATE

    @pl.when(c == 0)
    def _():
        xp_ref[0:hist, :] = cb0_ref[0]
        st_ref[0] = s0_ref[0]

    @pl.when(c > 0)
    def _():
        xp_ref[0:hist, :] = xp_ref[T:T + hist, :]

    _stage_rows(xp_ref, hist, xbc_ref[0], n_valid, T)
    conv = _causal_conv(xp_ref, cw_ref, hist - (SSD_CONV - 1), T, SSD_CONV) + cbias_ref[...]
    xa = _silu(conv)

    dtr = _softplus(dtT_ref[0] + dtb_ref[...])
    if n_valid < T:
        dtr = jnp.where(_iota(dtr.shape, 1) < n_valid, dtr, 0.0)
    a_r = dtr * (-jnp.exp(alog_ref[...]))
    tri_u = jnp.where(_iota((T, T), 0) <= _iota((T, T), 1), 1.0, 0.0)
    acum_r = _dot_hi(a_r, tri_u)
    atot = acum_r[:, T - 1:T]
    ecd = jnp.exp(atot)
    eac_r = jnp.exp(acum_r)
    dte_r = jnp.exp(atot - acum_r)
    rt = jnp.concatenate([dtr, acum_r, eac_r, dte_r], axis=0).T

    lo = _iota((T, LANES), 1) < SSD_HEAD_DIM
    rlo = _iota((LANES, LANES), 0) < SSD_HEAD_DIM
    causal = _iota((T, T), 0) >= _iota((T, T), 1)

    def col(q, h):
        return rt[:, q * nheads + h:q * nheads + h + 1]

    def pair(q, h0):
        return jnp.where(lo, col(q, h0), col(q, h0 + 1))

    hpg = nheads // ngrp
    for g in range(ngrp):
        bg = xa[:, d_inner + nst * g:d_inner + nst * (g + 1)]
        cg = xa[:, d_inner + nst * (ngrp + g):d_inner + nst * (ngrp + g + 1)]
        cb = _dot_nt(cg, bg)
        for jj in range(hpg // 2):
            h0 = g * hpg + 2 * jj
            sl = slice(SSD_HEAD_DIM * h0, SSD_HEAD_DIM * (h0 + 2))
            xs_p = xa[:, sl]
            x = xs_p * pair(0, h0)
            ms = []
            for h in (h0, h0 + 1):
                seg = col(1, h) - acum_r[h:h + 1, :]
                ms.append(cb * jnp.exp(jnp.where(causal, seg, NEG)))
            m = jnp.concatenate(ms, axis=1).astype(BF16)
            xbd = jnp.concatenate([jnp.where(lo, x, 0.0), jnp.where(lo, 0.0, x)], axis=0)
            ydiag = _dot(m, xbd)
            sp = st_ref[0, sl, :]
            yoff = _dot_nt(cg, sp) * pair(2, h0)
            ybuf_ref[:, sl] = ydiag + yoff + dch_ref[:, sl] * xs_p
            upd = _dot_tn(x * pair(3, h0), bg)
            cdp = jnp.where(rlo, ecd[h0:h0 + 1, :], ecd[h0 + 1:h0 + 2, :])
            st_ref[0, sl, :] = sp * cdp + upd

    y = ybuf_ref[0:n_valid, :] * _silu(z_ref[0])
    gw = d_inner // ngrp
    for g in range(ngrp):
        sl = slice(gw * g, gw * (g + 1))
        yg = y[:, sl]
        ms = jnp.mean(yg * yg, axis=-1, keepdims=True)
        y_ref[0, :, sl] = (yg * lax.rsqrt(ms + EPS) * ng_ref[:, sl]).astype(y_ref.dtype)


def _ssd_core(z, xbc, dt_t, s0, cb0, cw, cbias, dtb, alog, dch, ng, n_valid):
    b, l, d_inner = z.shape
    conv_dim = xbc.shape[-1]
    nheads = dt_t.shape[1]
    T = SSD_CHUNK
    nc = max(l // T, 1)
    tv = T if n_valid == T else n_valid
    tok = lambda n: pl.BlockSpec((1, tv, n), lambda i, c: (i, c, 0))
    full = lambda s: pl.BlockSpec((1,) + s, lambda i, c: (i, 0, 0))
    in_specs = [tok(d_inner), tok(conv_dim), pl.BlockSpec((1, nheads, T), lambda i, c: (i, 0, c)),
                full(s0.shape[1:]), full(cb0.shape[1:]), _const_spec(cw.shape), _const_spec(cbias.shape),
                _const_spec(dtb.shape), _const_spec(alog.shape), _const_spec(dch.shape), _const_spec(ng.shape)]
    kern = functools.partial(_ssd_core_kernel, T=T, n_valid=n_valid, nheads=nheads, d_inner=d_inner)
    return pl.pallas_call(
        kern, grid=(b, nc), in_specs=in_specs,
        out_specs=[tok(d_inner), full(s0.shape[1:])],
        out_shape=[jax.ShapeDtypeStruct((b, l, d_inner), _act_dtype(n_valid, T)),
                   jax.ShapeDtypeStruct(s0.shape, F32)],
        scratch_shapes=[pltpu.VMEM((T + SUBLANES, conv_dim), F32), pltpu.VMEM((T, d_inner), F32)],
        compiler_params=_cparams(("parallel", "arbitrary"), 48), name="ssd_core",
    )(z, xbc, dt_t, s0, cb0, cw, cbias, dtb, alog, dch, ng)


def _cnv_core_kernel(u_ref, cb0_ref, w_ref, b_ref, lg_ref, lb_ref, o_ref, xp_ref, *, T, n_valid, hist):
    c = pl.program_id(1)

    @pl.when(c == 0)
    def _():
        xp_ref[0:hist, :] = cb0_ref[0]

    @pl.when(c > 0)
    def _():
        xp_ref[0:hist, :] = xp_ref[T:T + hist, :]

    _stage_rows(xp_ref, hist, u_ref[0], n_valid, T)
    acc = _causal_conv(xp_ref, w_ref, hist - (CNV_WIDTH - 1), T, CNV_WIDTH) + b_ref[...]
    mu = jnp.mean(acc, axis=-1, keepdims=True)
    xc = acc - mu
    var = jnp.mean(xc * xc, axis=-1, keepdims=True)
    y = _silu(xc * lax.rsqrt(var + EPS) * lg_ref[...] + lb_ref[...])
    o_ref[0] = y[0:n_valid].astype(o_ref.dtype)


def _cnv_core(u, cb0, w, b, lg, lb, T, n_valid):
    bsz, l, cdim = u.shape
    hist = cb0.shape[1]
    nc = max(l // T, 1)
    tv = T if n_valid == T else n_valid
    tok = pl.BlockSpec((1, tv, cdim), lambda i, c: (i, c, 0))
    in_specs = [tok, pl.BlockSpec((1, hist, cdim), lambda i, c: (i, 0, 0)), _const_spec(w.shape),
                _const_spec(b.shape), _const_spec(lg.shape), _const_spec(lb.shape)]
    return pl.pallas_call(
        functools.partial(_cnv_core_kernel, T=T, n_valid=n_valid, hist=hist),
        grid=(bsz, nc), in_specs=in_specs, out_specs=tok,
        out_shape=jax.ShapeDtypeStruct((bsz, l, cdim), _act_dtype(n_valid, T)),
        scratch_shapes=[pltpu.VMEM((T + hist, cdim), F32)],
        compiler_params=_cparams(("parallel", "arbitrary")), name="cnv_core",
    )(u, cb0, w, b, lg, lb)


def _log_sigmoid_pair(z):
    lsz = jnp.minimum(z, 0.0) - jnp.log(1.0 + jnp.exp2(jnp.abs(z) * -math.log2(math.e)))
    return lsz, lsz - z


def _split_bf16(x):
    hi = x.astype(BF16)
    return hi, (x - hi.astype(F32)).astype(BF16)


def _sba_prompt_kernel(bias_ref, q_ref, k_ref, v_ref, o_ref, acc_ref, c0_ref, c1_ref, *, tq, tk):
    hp = pl.program_id(1)
    qi = pl.program_id(2)
    r = tq // tk
    hd = SBA_HEAD_DIM
    b0 = bias_ref[2 * hp]
    b1 = bias_ref[2 * hp + 1]
    q = q_ref[0].astype(F32) * hd ** -0.5
    lo_q = _iota((tq, LANES), 1) < hd
    q0 = jnp.where(lo_q, q, 0.0).astype(BF16)
    q1 = jnp.where(lo_q, 0.0, q).astype(BF16)
    lo_k = _iota((tk, LANES), 1) < hd
    rowi = _iota((2 * tk, 2 * tk), 0) % tk
    coli = _iota((2 * tk, 2 * tk), 1)
    cum_mat = jnp.where((coli >= tk) | (rowi > coli), 1.0, 0.0).astype(BF16)

    acc_ref[...] = jnp.zeros(acc_ref.shape, F32)
    c0_ref[...] = jnp.zeros(c0_ref.shape, F32)
    c1_ref[...] = jnp.zeros(c1_ref.shape, F32)

    def head(qh, bias, kb, c_ref, mask, row0):
        z = _dot_nt(qh[row0:], kb) + bias
        lsz, l1m = _log_sigmoid_pair(z)
        if mask is not None:
            l1m = jnp.where(mask, l1m, 0.0)
        hi, lo = _split_bf16(l1m)
        both = jnp.dot(jnp.concatenate([hi, lo], axis=1), cum_mat, preferred_element_type=F32)
        w = jnp.exp(lsz + both[:, :tk] + c_ref[row0:, :])
        c_ref[row0:, :] += both[:, tk:]
        if mask is not None:
            w = jnp.where(mask, w, 0.0)
        return w.astype(BF16)

    def block(start, mask, row0):
        kb = k_ref[0, pl.ds(start, tk), :]
        vb = v_ref[0, pl.ds(start, tk), :]
        w0 = head(q0, b0, kb, c0_ref, mask, row0)
        w1 = head(q1, b1, kb, c1_ref, mask, row0)
        vbd = jnp.concatenate([jnp.where(lo_k, vb, jnp.zeros_like(vb)),
                               jnp.where(lo_k, jnp.zeros_like(vb), vb)], axis=0)
        acc_ref[row0:, :] += jnp.dot(jnp.concatenate([w0, w1], axis=1), vbd, preferred_element_type=F32)

    for jj in reversed(range(r)):
        row0 = jj * tk
        mask = _iota((tq - row0, tk), 1) < _iota((tq - row0, tk), 0)
        block(pl.multiple_of((qi * r + jj) * tk, tk), mask, row0)

    def full_body(it, _):
        base = (qi - 1 - it) * r
        for jj in reversed(range(r)):
            block(pl.multiple_of((base + jj) * tk, tk), None, 0)
        return 0

    lax.fori_loop(0, qi, full_body, 0)
    o_ref[0] = acc_ref[...].astype(o_ref.dtype)


def _sba_prompt(q, k, v, bias):
    b, l, d = q.shape
    tk = SBA_BLOCK
    tq = min(4 * tk, l)
    nhp = d // LANES
    kv_spec = pl.BlockSpec((1, l, LANES), lambda i, h, j: (i, 0, h))
    q_spec = pl.BlockSpec((1, tq, LANES), lambda i, h, j: (i, j, h))
    return pl.pallas_call(
        functools.partial(_sba_prompt_kernel, tq=tq, tk=tk),
        grid=(b, nhp, l // tq),
        in_specs=[pl.BlockSpec(memory_space=pltpu.SMEM), q_spec, kv_spec, kv_spec],
        out_specs=q_spec,
        out_shape=jax.ShapeDtypeStruct((b, l, d), BF16),
        scratch_shapes=[pltpu.VMEM((tq, LANES), F32), pltpu.VMEM((tq, tk), F32), pltpu.VMEM((tq, tk), F32)],
        compiler_params=_cparams(("parallel", "parallel", "arbitrary")), name="sba_prompt",
    )(bias, q, k, v)


def _sba_sample_kernel(pt_ref, q_ref, kn_ref, vn_ref, bias_ref, *rest, nq, npages, page):
    kt_refs = rest[:npages]
    vt_refs = rest[npages:2 * npages]
    o_ref = rest[2 * npages]
    qpad_ref, knp_ref, vnp_ref, oacc_ref, c_ref = rest[2 * npages + 1:]
    hd = SBA_HEAD_DIM
    d = q_ref.shape[-1]
    nrow = LANES
    bd_mask = _iota((nrow, d), 0) // SUBLANES == _iota((nrow, d), 1) // hd
    rowi = _iota((2 * page, 2 * page), 0) % page
    coli = _iota((2 * page, 2 * page), 1)
    cum_mat = jnp.where((coli >= page) | (rowi > coli), 1.0, 0.0).astype(BF16)

    qpad_ref[...] = jnp.zeros(qpad_ref.shape, F32)
    qpad_ref[0:nq, :] = q_ref[0].astype(F32)
    q8 = qpad_ref[...] * (hd ** -0.5)
    qbd = jnp.where(bd_mask, jnp.concatenate([q8] * (nrow // SUBLANES), axis=0), 0.0).astype(BF16)
    knp_ref[...] = jnp.zeros(knp_ref.shape, F32)
    vnp_ref[...] = jnp.zeros(vnp_ref.shape, F32)
    knp_ref[0:nq, :] = kn_ref[0]
    vnp_ref[0:nq, :] = vn_ref[0]
    c_ref[...] = jnp.zeros(c_ref.shape, F32)

    def weights(z, mask):
        lsz, l1m = _log_sigmoid_pair(z + bias_ref[...])
        if mask is not None:
            l1m = jnp.where(mask, l1m, 0.0)
        hi, lo = _split_bf16(l1m)
        both = jnp.dot(jnp.concatenate([hi, lo], axis=1), cum_mat, preferred_element_type=F32)
        w = jnp.exp(lsz + both[:, :page] + c_ref[...])
        c_ref[...] += both[:, page:]
        if mask is not None:
            w = jnp.where(mask, w, 0.0)
        return w.astype(BF16)

    key_i = _iota((nrow, page), 1)
    slot = _iota((nrow, page), 0) % SUBLANES
    w = weights(_dot_nt(qbd, knp_ref[...]), (key_i < slot) & (key_i < nq))
    oacc_ref[...] = jnp.dot(w, vnp_ref[...].astype(BF16), preferred_element_type=F32)
    for p in reversed(range(npages)):
        w = weights(jnp.dot(qbd, kt_refs[p][0].astype(BF16), preferred_element_type=F32), None)
        oacc_ref[...] += _dot_nt(w, vt_refs[p][0])

    m = jnp.where(bd_mask, oacc_ref[...], 0.0)
    qpad_ref[...] = m.reshape(nrow // SUBLANES, SUBLANES, d).sum(axis=0)
    o_ref[0] = qpad_ref[0:nq, :].astype(o_ref.dtype)


def _sba_sample(q, kn, vn, cache_kt, cache_vt, page_table, bias_rows):
    b, nq, d = q.shape
    npages = page_table.shape[1]
    page = cache_kt.shape[2]
    tok = pl.BlockSpec((1, nq, d), lambda i, pt: (i, 0, 0))
    pages = [pl.BlockSpec((1, d, page), lambda i, pt, p=p: (pt[i, p], 0, 0)) for p in range(npages)]
    kern = functools.partial(_sba_sample_kernel, nq=nq, npages=npages, page=page)
    return pl.pallas_call(
        kern,
        grid_spec=pltpu.PrefetchScalarGridSpec(
            num_scalar_prefetch=1, grid=(b,),
            in_specs=[tok, tok, tok, pl.BlockSpec((LANES, page), lambda i, pt: (0, 0))] + pages + pages,
            out_specs=tok,
            scratch_shapes=[pltpu.VMEM((SUBLANES, d), F32), pltpu.VMEM((page, d), F32),
                            pltpu.VMEM((page, d), F32), pltpu.VMEM((LANES, d), F32),
                            pltpu.VMEM((LANES, page), F32)]),
        out_shape=jax.ShapeDtypeStruct((b, nq, d), F32),
        compiler_params=_cparams(("parallel",), 56), name="sba_sample",
    )(page_table, q, kn, vn, bias_rows, *([cache_kt] * npages), *([cache_vt] * npages))


def _gdn_core_kernel(qkv_ref, z_ref, ba_ref, s0_ref, cb0_ref, cw_ref, prm_ref, ng_ref,
                     y_ref, st_ref, xp_ref, obuf_ref, *, T, n_valid):
    c = pl.program_id(1)
    hist = SUBLANES
    nh, dk, dv = GDN_HEADS, GDN_DK, GDN_DV
    kdim = nh * dk
    T2 = 2 * T

    @pl.when(c == 0)
    def _():
        xp_ref[0:hist, :] = cb0_ref[0]
        st_ref[0] = s0_ref[0]

    @pl.when(c > 0)
    def _():
        xp_ref[0:hist, :] = xp_ref[T:T + hist, :]

    _stage_rows(xp_ref, hist, qkv_ref[0], n_valid, T)
    xa = _silu(_causal_conv(xp_ref, cw_ref, hist - (GDN_CONV - 1), T, GDN_CONV))

    npair = nh // 2
    lane2 = _iota((npair, T2), 1)
    beta_r = jax.nn.sigmoid(ba_ref[0, 0, 0:npair, :])
    g_r = -jnp.exp(prm_ref[1]) * _softplus(ba_ref[0, 0, npair:2 * npair, :] + prm_ref[0])
    if n_valid < T:
        valid = lane2 % T < n_valid
        beta_r = jnp.where(valid, beta_r, 0.0)
        g_r = jnp.where(valid, g_r, 0.0)
    same = _iota((T2, T2), 0) // T == _iota((T2, T2), 1) // T
    incl = same & (_iota((T2, T2), 0) >= _iota((T2, T2), 1))
    strict = same & (_iota((T2, T2), 0) > _iota((T2, T2), 1))
    cum_mat = jnp.where(same & (_iota((T2, T2), 0) <= _iota((T2, T2), 1)), 1.0, 0.0)
    g8 = jnp.concatenate([g_r, g_r], axis=0)
    gcum_r = _dot_hi(g8, cum_mat)[0:npair]
    gtot_r = jnp.where(lane2 < T, gcum_r[:, T - 1:T], gcum_r[:, T2 - 1:T2])
    eg_r = jnp.exp(gcum_r)
    rows = [beta_r, gcum_r, eg_r, beta_r * eg_r, jnp.exp(gtot_r - gcum_r)]
    pad = jnp.zeros((T2 - len(rows) * SUBLANES, T2), F32)
    rt = jnp.concatenate([jnp.concatenate([r, r], axis=0) for r in rows] + [pad], axis=0).T

    def col(qi, p):
        return rt[:, qi * SUBLANES + p:qi * SUBLANES + p + 1]

    eye = jnp.where(_iota((T2, T2), 0) == _iota((T2, T2), 1), 1.0, 0.0)

    def l2n(x):
        return x * lax.rsqrt(jnp.sum(x * x, axis=-1, keepdims=True) + EPS)

    pairs = range(npair)
    stacked = lambda off, w, p: jnp.concatenate(
        [xa[:, off + w * h:off + w * (h + 1)] for h in (2 * p, 2 * p + 1)], axis=0)
    qs = [jnp.concatenate([l2n(xa[:, dk * h:dk * (h + 1)]) for h in (2 * p, 2 * p + 1)], axis=0) * dk ** -0.5
          for p in pairs]
    ks = [jnp.concatenate([l2n(xa[:, kdim + dk * h:kdim + dk * (h + 1)]) for h in (2 * p, 2 * p + 1)], axis=0)
          for p in pairs]
    vs = [stacked(2 * kdim, dv, p) for p in pairs]
    cols = [[col(i, p) for i in range(5)] for p in pairs]
    decay = [jnp.exp(jnp.where(incl, cols[p][1] - gcum_r[p:p + 1, :], NEG)) for p in pairs]
    pw = [jnp.where(strict, cols[p][0] * _dot_nt(ks[p], ks[p]) * decay[p], 0.0) for p in pairs]
    tinv = [eye - pw[p] for p in pairs]
    for _ in range(int(math.log2(T)) - 1):
        pw = [_dot_split(pw[p], pw[p]) for p in pairs]
        tinv = [_dot_split(tinv[p], eye + pw[p]) for p in pairs]
    u = [_dot(tinv[p], vs[p] * cols[p][0]) for p in pairs]
    w = [_dot(tinv[p], ks[p] * cols[p][3]) for p in pairs]
    qk = [jnp.where(incl, _dot_nt(qs[p], ks[p]) * decay[p], 0.0) for p in pairs]
    q_dec = [qs[p] * cols[p][2] for p in pairs]
    k_dec = [ks[p] * cols[p][4] for p in pairs]
    half = lambda x, i: x[T * i:T * (i + 1)]
    s_in = [st_ref[0, dk * h:dk * (h + 1), :] for h in range(nh)]
    vnew = [jnp.concatenate([half(u[p], i) - _dot(half(w[p], i), s_in[2 * p + i]) for i in (0, 1)], axis=0)
            for p in pairs]
    o_intra = [_dot(qk[p], vnew[p]) for p in pairs]
    for h in range(nh):
        p, i = divmod(h, 2)
        obuf_ref[:, dv * h:dv * (h + 1)] = _dot(half(q_dec[p], i), s_in[h]) + half(o_intra[p], i)
    for h in range(nh):
        p, i = divmod(h, 2)
        cd = jnp.exp(gcum_r[p:p + 1, T * (i + 1) - 1:T * (i + 1)])
        st_ref[0, dk * h:dk * (h + 1), :] = s_in[h] * cd + _dot_tn(half(k_dec[p], i), half(vnew[p], i))

    o = obuf_ref[0:n_valid, :]
    z = z_ref[0]
    for h in range(nh):
        sl = slice(dv * h, dv * (h + 1))
        oh = o[:, sl]
        ms = jnp.mean(oh * oh, axis=-1, keepdims=True)
        y_ref[0, :, sl] = (oh * lax.rsqrt(ms + EPS) * ng_ref[...] * _silu(z[:, sl])).astype(y_ref.dtype)


def _gdn_core(qkv, z, ba_row, s0, cb0, cw, prm, ng, n_valid):
    b, l, qdim = qkv.shape
    vdim = z.shape[-1]
    T = GDN_CHUNK
    nc = max(l // T, 1)
    tv = T if n_valid == T else n_valid
    tok = lambda n: pl.BlockSpec((1, tv, n), lambda i, c: (i, c, 0))
    full = lambda s: pl.BlockSpec((1,) + s, lambda i, c: (i, 0, 0))
    in_specs = [tok(qdim), tok(vdim), pl.BlockSpec((1, 1) + ba_row.shape[2:], lambda i, c: (i, c, 0, 0)),
                full(s0.shape[1:]), full(cb0.shape[1:]), _const_spec(cw.shape), _const_spec(prm.shape),
                _const_spec(ng.shape)]
    return pl.pallas_call(
        functools.partial(_gdn_core_kernel, T=T, n_valid=n_valid),
        grid=(b, nc), in_specs=in_specs, out_specs=[tok(vdim), full(s0.shape[1:])],
        out_shape=[jax.ShapeDtypeStruct((b, l, vdim), _act_dtype(n_valid, T)),
                   jax.ShapeDtypeStruct(s0.shape, F32)],
        scratch_shapes=[pltpu.VMEM((T + SUBLANES, qdim), F32), pltpu.VMEM((T, vdim), F32)],
        compiler_params=_cparams(("parallel", "arbitrary")), name="gdn_core",
    )(qkv, z, ba_row, s0, cb0, cw, prm, ng)


def _hist_pad(buf, rows):
    return jnp.pad(buf, ((0, 0), (rows - buf.shape[1], 0), (0, 0)))


def _pair_rows(x, T):
    b, l, h = x.shape
    nc = l // T
    x = x.reshape(b, nc, T, h // 2, 2)
    return jnp.transpose(x, (0, 1, 3, 4, 2)).reshape(b, nc, h // 2, 2 * T)


def _pad_tokens(x, t):
    return jnp.pad(x, ((0, 0), (0, t - x.shape[1]), (0, 0)))


def kernel(x_prompt, x_sample, c_prompt, c_sample, state_ssd_conv, state_ssd, state_cnv_conv, cache_k, cache_v, page_table, state_gdn_conv, state_gdn, ada_w, ada_b, norm_mix_g, norm_ffn_g, ffn_w13, ffn_w2, ssd_w_in, ssd_conv_w, ssd_conv_b, ssd_dt_bias, ssd_A_log, ssd_D, ssd_norm_g, ssd_w_out, cnv_w_pw1, cnv_b_pw1, cnv_dw_w, cnv_dw_b, cnv_ln_g, cnv_ln_b, cnv_w_pw2, cnv_b_pw2, sba_w_qkv, sba_q_norm_g, sba_k_norm_g, sba_logit_bias, sba_w_o, gdn_w_in, gdn_conv_w, gdn_A_log, gdn_dt_bias, gdn_norm_g, gdn_w_out):
    bp, seq, d = x_prompt.shape
    bs, dseq, _ = x_sample.shape
    depth = ada_w.shape[0]
    ns = bs * dseq
    tm_p = 256
    tm_s = 256
    zero_d = jnp.zeros((d,), F32)

    nc_all = bp + bs
    mc = -(-nc_all // SUBLANES) * SUBLANES
    c_all = jnp.pad(jnp.concatenate([c_prompt, c_sample], axis=0), ((0, mc - nc_all), (0, 0)))
    mod = _adaln(c_all, ada_w, ada_b)

    def mods(layer):
        m = mod[layer]
        mp = m[:bp].reshape(bp, 1, 6, d)
        ms = jnp.broadcast_to(m[bp:nc_all].reshape(bs, 1, 6, d), (bs, dseq, 6, d)).reshape(1, ns, 6, d)
        return [mp[:, :, i] for i in range(6)], [ms[:, :, i] for i in range(6)]

    yp = x_prompt
    ys = x_sample.reshape(1, ns, d)
    outs = {}
    for layer in range(depth):
        kind = layer % 4
        (shp, scp, gp, shp2, scp2, gp2), (shs, scs, gs, shs2, scs2, gs2) = mods(layer)
        gmix = norm_mix_g[layer]
        streams = ((yp, scp, shp, tm_p), (ys, scs, shs, tm_s))
        bo = zero_d
        if kind == 0:
            d_inner = ssd_w_out.shape[0]
            conv_dim = ssd_conv_w.shape[1]
            nheads = ssd_dt_bias.shape[0]
            ws = [ssd_w_in[:, :d_inner].astype(BF16), ssd_w_in[:, d_inner:d_inner + conv_dim].astype(BF16),
                  ssd_w_in[:, d_inner + conv_dim:].astype(BF16)]
            odefs = [(d_inner, F32), (conv_dim, F32), (nheads, F32)]
            (zp, xbcp, dtp), (zs, xbcs, dts) = [
                _inproj(x, gmix, sc, sh, ws, None, [], _ep_identity, odefs, tm, "ssd_in")
                for x, sc, sh, tm in streams]
            prm = (ssd_conv_w, ssd_conv_b.reshape(1, conv_dim), ssd_dt_bias.reshape(nheads, 1),
                   ssd_A_log.reshape(nheads, 1), jnp.repeat(ssd_D, SSD_HEAD_DIM).reshape(1, d_inner),
                   ssd_norm_g.reshape(1, d_inner))
            st_shape = (SSD_HEAD_DIM * nheads, SSD_STATE)
            ap, stp = _ssd_core(zp, xbcp, jnp.swapaxes(dtp, 1, 2), jnp.zeros((bp,) + st_shape, F32),
                                jnp.zeros((bp, SUBLANES, conv_dim), F32), *prm, n_valid=SSD_CHUNK)
            zs, xbcs, dts = (t.reshape(bs, dseq, -1) for t in (zs, xbcs, dts))
            as_, sts = _ssd_core(zs, xbcs, jnp.swapaxes(_pad_tokens(dts, SSD_CHUNK), 1, 2),
                                 state_ssd.reshape((bs,) + st_shape), _hist_pad(state_ssd_conv, SUBLANES),
                                 *prm, n_valid=dseq)
            kc = SSD_CONV - 1
            outs["ssd"] = (xbcp[:, seq - kc:], jnp.concatenate([state_ssd_conv, xbcs], axis=1)[:, -kc:],
                           stp.reshape((bp,) + state_ssd.shape[1:]), sts.reshape(state_ssd.shape))
            wo = ssd_w_out
        elif kind == 1:
            cdim = cnv_dw_w.shape[1]
            ws = [cnv_w_pw1[:, :cdim].astype(BF16), cnv_w_pw1[:, cdim:].astype(BF16)]
            bsl = [cnv_b_pw1[:cdim].reshape(1, cdim), cnv_b_pw1[cdim:].reshape(1, cdim)]
            (up,), (us,) = [_inproj(x, gmix, sc, sh, ws, bsl, [], _ep_glu, [(cdim, F32)], tm, "cnv_in")
                            for x, sc, sh, tm in streams]
            hist = 4 * SUBLANES
            wpad = jnp.pad(cnv_dw_w, ((0, hist - cnv_dw_w.shape[0]), (0, 0)))
            prm = (wpad, cnv_dw_b.reshape(1, cdim), cnv_ln_g.reshape(1, cdim), cnv_ln_b.reshape(1, cdim))
            ap = _cnv_core(up, jnp.zeros((bp, hist, cdim), F32), *prm, T=256, n_valid=256)
            us = us.reshape(bs, dseq, cdim)
            as_ = _cnv_core(us, _hist_pad(state_cnv_conv, hist), *prm, T=SUBLANES, n_valid=dseq)
            kc = CNV_WIDTH - 1
            outs["cnv"] = (up[:, seq - kc:], jnp.concatenate([state_cnv_conv, us], axis=1)[:, -kc:])
            wo, bo = cnv_w_pw2, cnv_b_pw2
        elif kind == 2:
            nh = sba_logit_bias.shape[0]
            ws = [sba_w_qkv[:, i * d:(i + 1) * d].astype(BF16) for i in range(3)]
            ex = [jnp.tile(sba_q_norm_g, nh).reshape(1, d), jnp.tile(sba_k_norm_g, nh).reshape(1, d)]
            odefs = [(d, BF16), (d, F32), (d, F32), (d, BF16), (d, BF16)]
            (qp, kp, vp, kpb, vpb), (qs, ks, vs, _, _) = [
                _inproj(x, gmix, sc, sh, ws, None, ex, _ep_sba, odefs, tm, "sba_in")
                for x, sc, sh, tm in streams]
            ap = _sba_prompt(qp, kpb, vpb, sba_logit_bias)
            qs, ks, vs = (t.reshape(bs, dseq, d) for t in (qs, ks, vs))
            page = cache_k.shape[1]
            kt = jnp.transpose(cache_k, (0, 2, 3, 1)).reshape(-1, d, page)
            vt = jnp.transpose(cache_v, (0, 2, 3, 1)).reshape(-1, d, page)
            bias_rows = jnp.broadcast_to(jnp.repeat(sba_logit_bias, SUBLANES)[:, None], (LANES, page))
            as_ = _sba_sample(qs, ks, vs, kt, vt, page_table, bias_rows)
            hshape = (nh, SBA_HEAD_DIM)
            outs["sba"] = (kp.reshape((bp, seq) + hshape), vp.reshape((bp, seq) + hshape),
                           ks.reshape((bs, dseq) + hshape), vs.reshape((bs, dseq) + hshape))
            wo = sba_w_o
        else:
            qdim = gdn_conv_w.shape[1]
            vdim = gdn_w_out.shape[0]
            nh = GDN_HEADS
            ws = [gdn_w_in[:, :qdim].astype(BF16), gdn_w_in[:, qdim:qdim + vdim].astype(BF16),
                  gdn_w_in[:, qdim + vdim:].astype(BF16)]
            odefs = [(qdim, F32), (vdim, F32), (2 * nh, F32)]
            (qkvp, zp, bap), (qkvs, zs, bas) = [
                _inproj(x, gmix, sc, sh, ws, None, [], _ep_identity, odefs, tm, "gdn_in")
                for x, sc, sh, tm in streams]
            T = GDN_CHUNK
            prm_rows = jnp.stack([jnp.repeat(gdn_dt_bias.reshape(nh // 2, 2), T, axis=1),
                                  jnp.repeat(gdn_A_log.reshape(nh // 2, 2), T, axis=1)])

            def ba_rows(ba):
                return jnp.concatenate([_pair_rows(ba[..., :nh], T), _pair_rows(ba[..., nh:], T)], axis=2)

            prm = (gdn_conv_w, prm_rows, gdn_norm_g.reshape(1, GDN_DV))
            st_shape = (nh * GDN_DK, GDN_DV)
            ap, stp = _gdn_core(qkvp, zp, ba_rows(bap), jnp.zeros((bp,) + st_shape, F32),
                                jnp.zeros((bp, SUBLANES, qdim), F32), *prm, n_valid=T)
            qkvs, zs, bas = (t.reshape(bs, dseq, -1) for t in (qkvs, zs, bas))
            as_, sts = _gdn_core(qkvs, zs, ba_rows(_pad_tokens(bas, T)), state_gdn.reshape((bs,) + st_shape),
                                 _hist_pad(state_gdn_conv, SUBLANES), *prm, n_valid=dseq)
            kc = GDN_CONV - 1
            outs["gdn"] = (qkvp[:, seq - kc:], jnp.concatenate([state_gdn_conv, qkvs], axis=1)[:, -kc:],
                           stp.reshape((bp,) + state_gdn.shape[1:]), sts.reshape(state_gdn.shape))
            wo = gdn_w_out
        wo_b = wo.astype(BF16)
        w13 = ffn_w13[layer].astype(BF16)
        w2 = ffn_w2[layer].astype(BF16)
        g2 = norm_ffn_g[layer]
        yp = _post(yp, ap, wo_b, bo, gp, g2, scp2, shp2, gp2, w13, w2, tm_p, "post")
        ys = _post(ys, as_.reshape(1, ns, -1), wo_b, bo, gs, g2, scs2, shs2, gs2, w13, w2, tm_s, "post")

    ssd_conv_p, ssd_conv_s, ssd_state_p, ssd_state_s = outs["ssd"]
    cnv_conv_p, cnv_conv_s = outs["cnv"]
    k_p, v_p, k_s, v_s = outs["sba"]
    gdn_conv_p, gdn_conv_s, gdn_state_p, gdn_state_s = outs["gdn"]
    return (yp, ys.reshape(bs, dseq, d), ssd_conv_p, ssd_conv_s, ssd_state_p, ssd_state_s, cnv_conv_p,
            cnv_conv_s, k_p, v_p, k_s, v_s, gdn_conv_p, gdn_conv_s, gdn_state_p, gdn_state_s)
```
